```python
import math
import jax, jax.numpy as jnp
from jax import lax
import numpy as np

D_MODEL = 2048
BATCH = 8
SEQ = 2048
DEPTH = 1
DEC_BATCH = 16
DEC_SEQ = 2048
PAST_LEN = 128

HEAD_DIM = 128
NA_HEADS = 8
GQA_Q_HEADS = 8
GQA_KV_HEADS = 2
GRID_W = 64
NA_ROWS_MAX = 8
NA_COLS = 16
NA_QCOL_BLOCK = 16
NA_KCOL_BLOCK = NA_QCOL_BLOCK + NA_COLS
Q_BLOCK = 128
ROPE_THETA = 10000.0
PLE_DIM = 256
D_FF = -(-8 * D_MODEL // (3 * 256)) * 256
RMS_EPS = 1e-6
NEG_INF = -1e30

NA_WIDTH = NA_HEADS * HEAD_DIM
GQA_Q_WIDTH = GQA_Q_HEADS * HEAD_DIM
GQA_KV_WIDTH = GQA_KV_HEADS * HEAD_DIM
IN_COLS = 3 * NA_WIDTH + GQA_Q_WIDTH + 2 * GQA_KV_WIDTH + 2 * D_MODEL

kernel_name = "hybrid_natten_gqa_sandwich_encoder"


def rmsnorm(x, g):
    xf = x.astype(jnp.float32)
    y = xf * lax.rsqrt(jnp.mean(xf * xf, axis=-1, keepdims=True) + RMS_EPS)
    return (y * g.astype(jnp.float32)).astype(x.dtype)


def neighbourhood_attention(q, k, v, rpb):
    B, S, H, Dh = q.shape
    rows = S // GRID_W
    kh = min(NA_ROWS_MAX, rows)
    n_cb = GRID_W // NA_QCOL_BLOCK
    scale = 1.0 / math.sqrt(Dh)
    qc = np.arange(GRID_W).reshape(n_cb, NA_QCOL_BLOCK)
    cs = np.clip(qc - NA_COLS // 2, 0, GRID_W - NA_COLS)
    kc0 = np.clip(np.arange(n_cb) * NA_QCOL_BLOCK - NA_COLS // 2, 0, GRID_W - NA_KCOL_BLOCK)
    kc = kc0[:, None] + np.arange(NA_KCOL_BLOCK)
    col_ok = (kc[:, None, :] >= cs[..., None]) & (kc[:, None, :] < cs[..., None] + NA_COLS)
    dc_idx = np.clip(kc[:, None, :] - qc[..., None], -(NA_COLS - 1), NA_COLS - 1) + NA_COLS - 1
    col_bias = rpb.astype(jnp.float32)[:, :, dc_idx]
    col_ok = jnp.asarray(col_ok)[None, None, :, :, None, :]

    qg = q.reshape(B, rows, n_cb, NA_QCOL_BLOCK, H, Dh)
    kg = k.reshape(B, rows, GRID_W, H, Dh)[:, :, kc]
    vg = v.reshape(B, rows, GRID_W, H, Dh)[:, :, kc]

    def row_step(r):
        rs = jnp.clip(r - kh // 2, 0, rows - kh)
        q_r = lax.dynamic_index_in_dim(qg, r, axis=1, keepdims=False)
        k_r = lax.dynamic_slice_in_dim(kg, rs, kh, axis=1)
        v_r = lax.dynamic_slice_in_dim(vg, rs, kh, axis=1)
        s = jnp.einsum('bjqhd,bijkhd->bhjqik', q_r, k_r,
                       preferred_element_type=jnp.float32) * scale
        dr_idx = rs + jnp.arange(kh) - r + NA_ROWS_MAX - 1
        bias = jnp.take(col_bias, dr_idx, axis=1).transpose(0, 2, 3, 1, 4)
        s = jnp.where(col_ok, s + bias[None], NEG_INF)
        pr = jax.nn.softmax(s, axis=(-2, -1))
        return jnp.einsum('bhjqik,bijkhd->bjqhd', pr.astype(v.dtype), v_r)

    out = lax.map(row_step, jnp.arange(rows))
    return out.transpose(1, 0, 2, 3, 4, 5).reshape(B, S, H * Dh)


def axial_rope(x):
    S = x.shape[1]
    t = jnp.arange(S)
    half = x.shape[-1] // 2
    inv_freq = ROPE_THETA ** (-jnp.arange(0, half, 2, dtype=jnp.float32) / half)
    xf = x.astype(jnp.float32)

    def rot(xs, pos):
        ang = pos.astype(jnp.float32)[:, None] * inv_freq[None, :]
        cos = jnp.concatenate([jnp.cos(ang), jnp.cos(ang)], axis=-1)[None, :, None, :]
        sin = jnp.concatenate([jnp.sin(ang), jnp.sin(ang)], axis=-1)[None, :, None, :]
        x1, x2 = jnp.split(xs, 2, axis=-1)
        return xs * cos + jnp.concatenate([-x2, x1], axis=-1) * sin

    y = jnp.concatenate([rot(xf[..., :half], t // GRID_W), rot(xf[..., half:], t % GRID_W)], axis=-1)
    return y.astype(x.dtype)


def gqa_attention(q, k, v):
    B, S, Hq, Dh = q.shape
    Hkv = k.shape[2]
    G = Hq // Hkv
    nb = S // Q_BLOCK
    scale = 1.0 / math.sqrt(Dh)
    qb = q.reshape(B, nb, Q_BLOCK, Hkv, G, Dh).transpose(1, 0, 2, 3, 4, 5)

    def block(q_blk):
        s = jnp.einsum('bqhgd,bkhd->bhgqk', q_blk, k, preferred_element_type=jnp.float32) * scale
        pr = jax.nn.softmax(s, axis=-1)
        return jnp.einsum('bhgqk,bkhd->bqhgd', pr.astype(v.dtype), v)

    o = lax.map(block, qb)
    return o.transpose(1, 0, 2, 3, 4, 5).reshape(B, S, Hq * Dh)


def encoder_layer(h, p_l, g_pre_mix, w_in, rpb, g_q, g_k, w_branch_a, w_branch_b, w_out,
                  g_post_mix, g_pre_ffn, w_ffn_in, w_ffn_out, g_post_ffn, w_ple, w_ple_gate, g_ple):
    B, S, _ = h.shape
    xn = rmsnorm(h, g_pre_mix)
    proj = xn @ w_in
    offs = np.cumsum([NA_WIDTH, NA_WIDTH, NA_WIDTH, GQA_Q_WIDTH, GQA_KV_WIDTH, GQA_KV_WIDTH, D_MODEL])
    qa, ka, va, qb, kb, vb, gate_a, gate_b = jnp.split(proj, offs, axis=-1)
    qa = qa.reshape(B, S, NA_HEADS, HEAD_DIM)
    ka = ka.reshape(B, S, NA_HEADS, HEAD_DIM)
    va = va.reshape(B, S, NA_HEADS, HEAD_DIM)
    qb = axial_rope(rmsnorm(qb.reshape(B, S, GQA_Q_HEADS, HEAD_DIM), g_q))
    kb = axial_rope(rmsnorm(kb.reshape(B, S, GQA_KV_HEADS, HEAD_DIM), g_k))
    vb = vb.reshape(B, S, GQA_KV_HEADS, HEAD_DIM)
    ya = neighbourhood_attention(qa, ka, va, rpb) @ w_branch_a
    yb = gqa_attention(qb, kb, vb) @ w_branch_b
    mix = (jax.nn.sigmoid(gate_a) * ya + jax.nn.sigmoid(gate_b) * yb) @ w_out
    h = h + rmsnorm(mix, g_post_mix)
    gu = rmsnorm(h, g_pre_ffn) @ w_ffn_in
    g, u = jnp.split(gu, 2, axis=-1)
    h = h + rmsnorm((jax.nn.silu(g) * u) @ w_ffn_out, g_post_ffn)
    e = p_l @ w_ple
    h = h + rmsnorm(jax.nn.sigmoid(h @ w_ple_gate) * e, g_ple)
    return h


def setup_inputs(seed: int = 0) -> dict:
    key = jax.random.key(seed)
    ks = jax.random.split(key, 24)
    f32 = jnp.float32

    def w(k, shape, fan_in):
        return jax.random.normal(k, shape, f32) * fan_in ** -0.5

    def gain(k, shape):
        return 1.0 + 0.1 * jax.random.normal(k, shape, f32)

    return {
        "x_prompt": jax.random.normal(ks[0], (BATCH, SEQ, D_MODEL), f32),
        "x_sample": jax.random.normal(ks[1], (DEC_BATCH, DEC_SEQ, D_MODEL), f32),
        "p_prompt": jax.random.normal(ks[2], (DEPTH, BATCH, SEQ, PLE_DIM), f32),
        "p_sample": jax.random.normal(ks[3], (DEPTH, DEC_BATCH, DEC_SEQ, PLE_DIM), f32),
        "g_pre_mix": gain(ks[4], (DEPTH, D_MODEL)),
        "w_in": w(ks[5], (DEPTH, D_MODEL, IN_COLS), D_MODEL),
        "rpb": 0.1 * jax.random.normal(ks[6], (DEPTH, NA_HEADS, 2 * NA_ROWS_MAX - 1, 2 * NA_COLS - 1), f32),
        "g_q": gain(ks[7], (DEPTH, HEAD_DIM)),
        "g_k": gain(ks[8], (DEPTH, HEAD_DIM)),
        "w_branch_a": w(ks[9], (DEPTH, NA_WIDTH, D_MODEL), NA_WIDTH),
        "w_branch_b": w(ks[10], (DEPTH, GQA_Q_WIDTH, D_MODEL), GQA_Q_WIDTH),
        "w_out": w(ks[11], (DEPTH, D_MODEL, D_MODEL), D_MODEL),
        "g_post_mix": gain(ks[12], (DEPTH, D_MODEL)),
        "g_pre_ffn": gain(ks[13], (DEPTH, D_MODEL)),
        "w_ffn_in": w(ks[14], (DEPTH, D_MODEL, 2 * D_FF), D_MODEL),
        "w_ffn_out": w(ks[15], (DEPTH, D_FF, D_MODEL), D_FF),
        "g_post_ffn": gain(ks[16], (DEPTH, D_MODEL)),
        "w_ple": w(ks[17], (DEPTH, PLE_DIM, D_MODEL), PLE_DIM),
        "w_ple_gate": w(ks[18], (DEPTH, D_MODEL, D_MODEL), D_MODEL),
        "g_ple": gain(ks[19], (DEPTH, D_MODEL)),
    }


def reference(x_prompt, x_sample, p_prompt, p_sample, g_pre_mix, w_in, rpb, g_q, g_k,
              w_branch_a, w_branch_b, w_out, g_post_mix, g_pre_ffn, w_ffn_in, w_ffn_out,
              g_post_ffn, w_ple, w_ple_gate, g_ple):
    h_p = x_prompt
    h_s = x_sample
    for i in range(DEPTH):
        lw = (g_pre_mix[i], w_in[i], rpb[i], g_q[i], g_k[i], w_branch_a[i], w_branch_b[i], w_out[i],
              g_post_mix[i], g_pre_ffn[i], w_ffn_in[i], w_ffn_out[i], g_post_ffn[i],
              w_ple[i], w_ple_gate[i], g_ple[i])
        h_p = encoder_layer(h_p, p_prompt[i], *lw)
        h_s = encoder_layer(h_s, p_sample[i], *lw)
    y_prompt = h_p
    y_sample = h_s
    return (y_prompt, y_sample)
```

```python
import functools
import math

import numpy as np
import jax
import jax.numpy as jnp
from jax import lax
from jax.experimental import pallas as pl
from jax.experimental.pallas import tpu as pltpu

F32 = jnp.float32
BF16 = jnp.bfloat16

HEAD_DIM = 128
NA_HEADS = 8
GQA_Q_HEADS = 8
GQA_KV_HEADS = 2
GQA_GROUP = GQA_Q_HEADS // GQA_KV_HEADS
GRID_W = 64
NA_ROWS = 8
NA_COLS = 16
ROPE_THETA = 10000.0
RMS_EPS = 1e-6
NEG_INF = -1e30
ATTN_SCALE = 1.0 / math.sqrt(HEAD_DIM)

NA_WIDTH = NA_HEADS * HEAD_DIM
GQA_Q_WIDTH = GQA_Q_HEADS * HEAD_DIM
GQA_KV_WIDTH = GQA_KV_HEADS * HEAD_DIM

PROJ_TN = 512
COL_GATE_A = 0
COL_GATE_B = 2048
COL_QA = 4096
COL_KA = COL_QA + NA_WIDTH
COL_VA = COL_KA + NA_WIDTH
COL_QB = COL_VA + NA_WIDTH
COL_KB = COL_QB + GQA_Q_WIDTH
COL_VB = COL_KB + GQA_KV_WIDTH
PROJ_COLS = COL_VB + GQA_KV_WIDTH

NA_QROWS = 4
NA_WIN_ROWS = 12
NA_TQ = NA_QROWS * GRID_W
NA_TK = NA_WIN_ROWS * GRID_W

VMEM_LIMIT = 56 * 1024 * 1024


def _rms(x, g):
    ms = jnp.mean(x * x, axis=-1, keepdims=True)
    return x * lax.rsqrt(ms + RMS_EPS) * g


def _sigmoid(x):
    return 1.0 / (1.0 + jnp.exp(-x))


def _rope(a, cos, sin_lo, sin_hi):
    return a * cos + pltpu.roll(a, 96, 1) * sin_lo + pltpu.roll(a, 32, 1) * sin_hi


def _inproj_kernel(x_ref, g_ref, w_ref, cos_ref, slo_ref, shi_ref, gq_ref, gk_ref,
                   o_ref, xn_ref):
    j = pl.program_id(1)

    @pl.when(j == 0)
    def _():
        xn_ref[...] = _rms(x_ref[...], g_ref[...]).astype(BF16)

    acc = jnp.dot(xn_ref[...], w_ref[...], preferred_element_type=F32)

    j_qa = COL_QA // PROJ_TN
    j_ka = COL_KA // PROJ_TN
    j_qb = COL_QB // PROJ_TN
    j_kb = COL_KB // PROJ_TN

    @pl.when(j < j_qa)
    def _():
        o_ref[...] = _sigmoid(acc).astype(BF16)

    @pl.when((j >= j_qa) & (j < j_ka))
    def _():
        o_ref[...] = (acc * ATTN_SCALE).astype(BF16)

    @pl.when((j >= j_ka) & (j < j_qb))
    def _():
        o_ref[...] = acc.astype(BF16)

    def norm_rope(a, g):
        a = _rms(a, g)
        return _rope(a, cos_ref[...], slo_ref[...], shi_ref[...])

    @pl.when((j >= j_qb) & (j < j_kb))
    def _():
        for h in range(PROJ_TN // HEAD_DIM):
            sl = slice(h * HEAD_DIM, (h + 1) * HEAD_DIM)
            o_ref[:, sl] = (norm_rope(acc[:, sl], gq_ref[...]) * ATTN_SCALE).astype(BF16)

    @pl.when(j == j_kb)
    def _():
        for h in range(GQA_KV_HEADS):
            sl = slice(h * HEAD_DIM, (h + 1) * HEAD_DIM)
            o_ref[:, sl] = norm_rope(acc[:, sl], gk_ref[...]).astype(BF16)
        o_ref[:, GQA_KV_WIDTH:] = acc[:, GQA_KV_WIDTH:].astype(BF16)


def _in_proj(x2d, g_pre, w_in, cos, sin_lo, sin_hi, g_q, g_k, seq):
    t, d = x2d.shape
    tm = 1024
    n_pos_blocks = seq // tm
    return pl.pallas_call(
        _inproj_kernel,
        out_shape=jax.ShapeDtypeStruct((t, PROJ_COLS), BF16),
        grid=(t // tm, PROJ_COLS // PROJ_TN),
        in_specs=[
            pl.BlockSpec((tm, d), lambda i, j: (i, 0)),
            pl.BlockSpec((1, d), lambda i, j: (0, 0)),
            pl.BlockSpec((d, PROJ_TN), lambda i, j: (0, j)),
            pl.BlockSpec((tm, HEAD_DIM), lambda i, j: (i % n_pos_blocks, 0)),
            pl.BlockSpec((tm, HEAD_DIM), lambda i, j: (i % n_pos_blocks, 0)),
            pl.BlockSpec((tm, HEAD_DIM), lambda i, j: (i % n_pos_blocks, 0)),
            pl.BlockSpec((1, HEAD_DIM), lambda i, j: (0, 0)),
            pl.BlockSpec((1, HEAD_DIM), lambda i, j: (0, 0)),
        ],
        out_specs=pl.BlockSpec((tm, PROJ_TN), lambda i, j: (i, j)),
        scratch_shapes=[pltpu.VMEM((tm, d), BF16)],
        compiler_params=pltpu.CompilerParams(
            dimension_semantics=("parallel", "arbitrary"),
            vmem_limit_bytes=VMEM_LIMIT),
        name="in_proj",
    )(x2d, g_pre, w_in, cos, sin_lo, sin_hi, g_q, g_k)


def _softmax_pv(s, v):
    m = jnp.max(s, axis=-1, keepdims=True)
    p = jnp.exp(s - m)
    l = jnp.sum(p, axis=-1, keepdims=True)
    o = jnp.dot(p.astype(BF16), v, preferred_element_type=F32)
    return o * (1.0 / l)


def _na_window_start(blk, rows):
    return jnp.clip(blk * NA_QROWS - NA_ROWS // 2, 0, rows - NA_WIN_ROWS)


def _na_kernel(q_ref, k_ref, v_ref, tab_ref, o_ref, *, rows):
    blk = pl.program_id(1)
    k0 = pl.multiple_of(_na_window_start(blk, rows) * GRID_W, GRID_W)
    for h in range(NA_HEADS):
        sl = slice(h * HEAD_DIM, (h + 1) * HEAD_DIM)
        q = q_ref[:, sl]
        k = k_ref[pl.ds(k0, NA_TK), sl]
        v = v_ref[pl.ds(k0, NA_TK), sl]
        s = lax.dot_general(q, k, (((1,), (1,)), ((), ())), preferred_element_type=F32)
        s = s + tab_ref[0, h]
        o_ref[:, sl] = _softmax_pv(s, v).astype(BF16)


def _na_bias_tables(rpb, rows):
    n_blk = rows // NA_QROWS
    tabs = []
    for blk in (0, 1, n_blk - 1):
        r0 = blk * NA_QROWS
        k0 = int(np.clip(r0 - NA_ROWS // 2, 0, rows - NA_WIN_ROWS))
        qr = (r0 + np.arange(NA_QROWS))[:, None] + np.zeros((1, GRID_W), np.int64)
        qc = np.zeros((NA_QROWS, 1), np.int64) + np.arange(GRID_W)[None, :]
        kr = (k0 + np.arange(NA_WIN_ROWS))[:, None] + np.zeros((1, GRID_W), np.int64)
        kc = np.zeros((NA_WIN_ROWS, 1), np.int64) + np.arange(GRID_W)[None, :]
        qr, qc, kr, kc = qr.reshape(-1, 1), qc.reshape(-1, 1), kr.reshape(1, -1), kc.reshape(1, -1)
        rs = np.clip(qr - NA_ROWS // 2, 0, rows - NA_ROWS)
        cs = np.clip(qc - NA_COLS // 2, 0, GRID_W - NA_COLS)
        ok = (kr >= rs) & (kr < rs + NA_ROWS) & (kc >= cs) & (kc < cs + NA_COLS)
        dr = np.clip(kr - qr + NA_ROWS - 1, 0, 2 * NA_ROWS - 2)
        dc = np.clip(kc - qc, -(NA_COLS - 1), NA_COLS - 1) + NA_COLS - 1
        bias = rpb.astype(F32)[:, dr, dc]
        tabs.append(jnp.where(jnp.asarray(ok)[None], bias, NEG_INF))
    return jnp.stack(tabs)


def _na_attention(proj, tables, batch, seq):
    rows = seq // GRID_W
    n_blk = rows // NA_QROWS
    t = batch * seq

    def tab_index(b, blk):
        return (jnp.where(blk == 0, 0, jnp.where(blk == n_blk - 1, 2, 1)), 0, 0, 0)

    return pl.pallas_call(
        functools.partial(_na_kernel, rows=rows),
        out_shape=jax.ShapeDtypeStruct((t, NA_WIDTH), BF16),
        grid=(batch, n_blk),
        in_specs=[
            pl.BlockSpec((NA_TQ, NA_WIDTH), lambda b, blk: (b * n_blk + blk, COL_QA // NA_WIDTH)),
            pl.BlockSpec((seq, NA_WIDTH), lambda b, blk: (b, COL_KA // NA_WIDTH)),
            pl.BlockSpec((seq, NA_WIDTH), lambda b, blk: (b, COL_VA // NA_WIDTH)),
            pl.BlockSpec((1, NA_HEADS, NA_TQ, NA_TK), tab_index),
        ],
        out_specs=pl.BlockSpec((NA_TQ, NA_WIDTH), lambda b, blk: (b * n_blk + blk, 0)),
        compiler_params=pltpu.CompilerParams(
            dimension_semantics=("parallel", "arbitrary"),
            vmem_limit_bytes=VMEM_LIMIT),
        name="na_attn",
    )(proj, proj, proj, tables)


GQA_TQ = 256


def _gqa_kernel(q_ref, k_ref, v_ref, o_ref):
    q = jnp.concatenate(
        [q_ref[:, h * HEAD_DIM:(h + 1) * HEAD_DIM] for h in range(GQA_GROUP)], axis=0)
    s = lax.dot_general(q, k_ref[...], (((1,), (1,)), ((), ())), preferred_element_type=F32)
    o = _softmax_pv(s, v_ref[...]).astype(BF16)
    for h in range(GQA_GROUP):
        o_ref[:, h * HEAD_DIM:(h + 1) * HEAD_DIM] = o[h * GQA_TQ:(h + 1) * GQA_TQ, :]


def _gqa_attention(proj, batch, seq):
    t = batch * seq
    n_q = seq // GQA_TQ
    gw = GQA_GROUP * HEAD_DIM
    return pl.pallas_call(
        _gqa_kernel,
        out_shape=jax.ShapeDtypeStruct((t, GQA_Q_WIDTH), BF16),
        grid=(batch, GQA_KV_HEADS, n_q),
        in_specs=[
            pl.BlockSpec((GQA_TQ, gw), lambda b, g, i: (b * n_q + i, COL_QB // gw + g)),
            pl.BlockSpec((seq, HEAD_DIM), lambda b, g, i: (b, COL_KB // HEAD_DIM + g)),
            pl.BlockSpec((seq, HEAD_DIM), lambda b, g, i: (b, COL_VB // HEAD_DIM + g)),
        ],
        out_specs=pl.BlockSpec((GQA_TQ, gw), lambda b, g, i: (b * n_q + i, g)),
        compiler_params=pltpu.CompilerParams(
            dimension_semantics=("parallel", "parallel", "arbitrary"),
            vmem_limit_bytes=VMEM_LIMIT),
        name="gqa_attn",
    )(proj, proj, proj)


def _mix_kernel(a_ref, b_ref, ga_ref, gb_ref, x_ref, wa_ref, wb_ref, wo_ref,
                gpost_ref, gpre_ref, h_ref, xn_ref):
    ya = jnp.dot(a_ref[...], wa_ref[...], preferred_element_type=F32)
    yb = jnp.dot(b_ref[...], wb_ref[...], preferred_element_type=F32)
    mix = ga_ref[...].astype(F32) * ya + gb_ref[...].astype(F32) * yb
    m2 = jnp.dot(mix.astype(BF16), wo_ref[...], preferred_element_type=F32)
    h = x_ref[...] + _rms(m2, gpost_ref[...])
    h_ref[...] = h
    xn_ref[...] = _rms(h, gpre_ref[...]).astype(BF16)


def _resident(shape):
    return pl.BlockSpec(shape, lambda *_: (0,) * len(shape), pipeline_mode=pl.Buffered(1))


def _mix(na_out, gqa_out, proj, x2d, wa, wb, wo, g_post, g_pre_ffn):
    t, d = x2d.shape
    tm = 256
    return pl.pallas_call(
        _mix_kernel,
        out_shape=(jax.ShapeDtypeStruct((t, d), F32), jax.ShapeDtypeStruct((t, d), BF16)),
        grid=(t // tm,),
        in_specs=[
            pl.BlockSpec((tm, NA_WIDTH), lambda i: (i, 0)),
            pl.BlockSpec((tm, GQA_Q_WIDTH), lambda i: (i, 0)),
            pl.BlockSpec((tm, d), lambda i: (i, COL_GATE_A // d)),
            pl.BlockSpec((tm, d), lambda i: (i, COL_GATE_B // d)),
            pl.BlockSpec((tm, d), lambda i: (i, 0)),
            _resident(wa.shape),
            _resident(wb.shape),
            _resident(wo.shape),
            _resident((1, d)),
            _resident((1, d)),
        ],
        out_specs=(pl.BlockSpec((tm, d), lambda i: (i, 0)), pl.BlockSpec((tm, d), lambda i: (i, 0))),
        compiler_params=pltpu.CompilerParams(
            dimension_semantics=("parallel",),
            vmem_limit_bytes=VMEM_LIMIT),
        name="mix",
    )(na_out, gqa_out, proj, proj, x2d, wa, wb, wo, g_post, g_pre_ffn)


FFN_TF = 512


def _ffn_kernel(xn_ref, wg_ref, wu_ref, wo_ref, f_ref, acc_ref):
    j = pl.program_id(1)
    xn = xn_ref[...]
    g = jnp.dot(xn, wg_ref[...], preferred_element_type=F32)
    u = jnp.dot(xn, wu_ref[...], preferred_element_type=F32)
    act = (g * _sigmoid(g) * u).astype(BF16)
    part = jnp.dot(act, wo_ref[...], preferred_element_type=F32)

    @pl.when(j == 0)
    def _():
        acc_ref[...] = part

    @pl.when(j > 0)
    def _():
        acc_ref[...] += part

    @pl.when(j == pl.num_programs(1) - 1)
    def _():
        f_ref[...] = acc_ref[...].astype(BF16)


def _ffn(xn2, w_ffn_in, w_ffn_out):
    t, d = xn2.shape
    d_ff = w_ffn_out.shape[0]
    tm = 1024
    n_f = d_ff // FFN_TF
    return pl.pallas_call(
        _ffn_kernel,
        out_shape=jax.ShapeDtypeStruct((t, d), BF16),
        grid=(t // tm, n_f),
        in_specs=[
            pl.BlockSpec((tm, d), lambda i, j: (i, 0)),
            pl.BlockSpec((d, FFN_TF), lambda i, j: (0, j)),
            pl.BlockSpec((d, FFN_TF), lambda i, j: (0, n_f + j)),
            pl.BlockSpec((FFN_TF, d), lambda i, j: (j, 0)),
        ],
        out_specs=pl.BlockSpec((tm, d), lambda i, j: (i, 0)),
        scratch_shapes=[pltpu.VMEM((tm, d), F32)],
        compiler_params=pltpu.CompilerParams(
            dimension_semantics=("parallel", "arbitrary"),
            vmem_limit_bytes=VMEM_LIMIT),
        name="ffn",
    )(xn2, w_ffn_in, w_ffn_in, w_ffn_out)


def _ple_kernel(h_ref, f_ref, p_ref, wple_ref, wgate_ref, gffn_ref, gple_ref, y_ref):
    h = h_ref[...] + _rms(f_ref[...].astype(F32), gffn_ref[...])
    e = jnp.dot(p_ref[...].astype(BF16), wple_ref[...], preferred_element_type=F32)
    gate = _sigmoid(jnp.dot(h.astype(BF16), wgate_ref[...], preferred_element_type=F32))
    y_ref[...] = h + _rms(gate * e, gple_ref[...])


def _ple(h1, f, p2d, w_ple, w_gate, g_ffn, g_ple):
    t, d = h1.shape
    pd = p2d.shape[1]
    tm = 256
    return pl.pallas_call(
        _ple_kernel,
        out_shape=jax.ShapeDtypeStruct((t, d), F32),
        grid=(t // tm,),
        in_specs=[
            pl.BlockSpec((tm, d), lambda i: (i, 0)),
            pl.BlockSpec((tm, d), lambda i: (i, 0)),
            pl.BlockSpec((tm, pd), lambda i: (i, 0)),
            _resident(w_ple.shape),
            _resident(w_gate.shape),
            _resident((1, d)),
            _resident((1, d)),
        ],
        out_specs=pl.BlockSpec((tm, d), lambda i: (i, 0)),
        compiler_params=pltpu.CompilerParams(
            dimension_semantics=("parallel",),
            vmem_limit_bytes=VMEM_LIMIT),
        name="ple",
    )(h1, f, p2d, w_ple, w_gate, g_ffn, g_ple)


def _rope_tables(seq):
    half = HEAD_DIM // 2
    t = jnp.arange(seq)
    inv_freq = ROPE_THETA ** (-jnp.arange(0, half, 2, dtype=F32) / half)

    def tables(pos):
        ang = pos.astype(F32)[:, None] * inv_freq[None, :]
        return (jnp.concatenate([jnp.cos(ang), jnp.cos(ang)], axis=-1),
                jnp.concatenate([jnp.sin(ang), jnp.sin(ang)], axis=-1))

    cos_r, sin_r = tables(t // GRID_W)
    cos_c, sin_c = tables(t % GRID_W)
    cos = jnp.concatenate([cos_r, cos_c], axis=-1)
    sin = jnp.concatenate([sin_r, sin_c], axis=-1)
    first = (jnp.arange(HEAD_DIM) % half) < half // 2
    sin_lo = jnp.where(first[None, :], -sin, 0.0)
    sin_hi = jnp.where(first[None, :], 0.0, sin)
    return cos, sin_lo, sin_hi


def _layer(x, p, w, seq):
    batch = x.shape[0]
    d = x.shape[-1]
    x2d = x.reshape(batch * seq, d)
    p2d = p.reshape(batch * seq, p.shape[-1])
    proj = _in_proj(x2d, w["g_pre_mix"], w["w_in"], w["cos"], w["sin_lo"], w["sin_hi"],
                    w["g_q"], w["g_k"], seq)
    na_out = _na_attention(proj, w["na_tables"], batch, seq)
    gqa_out = _gqa_attention(proj, batch, seq)
    h1, xn2 = _mix(na_out, gqa_out, proj, x2d, w["w_branch_a"], w["w_branch_b"], w["w_out"],
                   w["g_post_mix"], w["g_pre_ffn"])
    f = _ffn(xn2, w["w_ffn_in"], w["w_ffn_out"])
    y = _ple(h1, f, p2d, w["w_ple"], w["w_ple_gate"], w["g_post_ffn"], w["g_ple"])
    return y.reshape(x.shape)


def kernel(x_prompt, x_sample, p_prompt, p_sample, g_pre_mix, w_in, rpb, g_q, g_k, w_branch_a, w_branch_b, w_out, g_post_mix, g_pre_ffn, w_ffn_in, w_ffn_out, g_post_ffn, w_ple, w_ple_gate, g_ple):
    depth = w_in.shape[0]
    h_p, h_s = x_prompt, x_sample
    for i in range(depth):
        n_attn_cols = 3 * NA_WIDTH + GQA_Q_WIDTH + 2 * GQA_KV_WIDTH
        w_in_i = jnp.concatenate([w_in[i][:, n_attn_cols:], w_in[i][:, :n_attn_cols]], axis=1)
        w = {
            "w_in": w_in_i.astype(BF16),
            "w_branch_a": w_branch_a[i].astype(BF16),
            "w_branch_b": w_branch_b[i].astype(BF16),
            "w_out": w_out[i].astype(BF16),
            "w_ffn_in": w_ffn_in[i].astype(BF16),
            "w_ffn_out": w_ffn_out[i].astype(BF16),
            "w_ple": w_ple[i].astype(BF16),
            "w_ple_gate": w_ple_gate[i].astype(BF16),
            "g_pre_mix": g_pre_mix[i][None, :],
            "g_q": g_q[i][None, :],
            "g_k": g_k[i][None, :],
            "g_post_mix": g_post_mix[i][None, :],
            "g_pre_ffn": g_pre_ffn[i][None, :],
            "g_post_ffn": g_post_ffn[i][None, :],
            "g_ple": g_ple[i][None, :],
        }
        outs = []
        for h, p in ((h_p, p_prompt[i]), (h_s, p_sample[i])):
            seq = h.shape[1]
            w["cos"], w["sin_lo"], w["sin_hi"] = _rope_tables(seq)
            w["na_tables"] = _na_bias_tables(rpb[i], seq // GRID_W)
            outs.append(_layer(h, p, w, seq))
        h_p, h_s = outs
    return (h_p, h_s)
```

```python
import functools
import math

import numpy as np
import jax
import jax.numpy as jnp
from jax import lax
from jax.experimental import pallas as pl
from jax.experimental.pallas import tpu as pltpu

F32 = jnp.float32
BF16 = jnp.bfloat16

HEAD_DIM = 128
NA_HEADS = 8
GQA_Q_HEADS = 8
GQA_KV_HEADS = 2
GQA_GROUP = GQA_Q_HEADS // GQA_KV_HEADS
GRID_W = 64
NA_ROWS = 8
NA_COLS = 16
ROPE_THETA = 10000.0
RMS_EPS = 1e-6
NEG_INF = -1e30
LOG2E = math.log2(math.e)
Q_SCALE = LOG2E / math.sqrt(HEAD_DIM)

NA_WIDTH = NA_HEADS * HEAD_DIM
GQA_Q_WIDTH = GQA_Q_HEADS * HEAD_DIM
GQA_KV_WIDTH = GQA_KV_HEADS * HEAD_DIM
N_ATTN_A = 3 * NA_WIDTH
N_ATTN_B = GQA_Q_WIDTH + 2 * GQA_KV_WIDTH

NA_QROWS = 4
NA_WIN_ROWS = 12
NA_TQ = NA_QROWS * GRID_W
NA_TK = NA_WIN_ROWS * GRID_W

VMEM_LIMIT = 56 * 1024 * 1024


def _rms(x, g):
    ms = jnp.mean(x * x, axis=-1, keepdims=True)
    return x * lax.rsqrt(ms + RMS_EPS) * g


def _sigmoid(x):
    return 1.0 / (1.0 + jnp.exp(-x))


def _resident(shape):
    return pl.BlockSpec(shape, lambda *_: (0,) * len(shape), pipeline_mode=pl.Buffered(1))


def _rope(a, cos, sin_lo, sin_hi):
    return a * cos + pltpu.roll(a, 96, 1) * sin_lo + pltpu.roll(a, 32, 1) * sin_hi


def _qkv_b_kernel(x_ref, g_ref, w_ref, cos_ref, slo_ref, shi_ref, gq_ref, gk_ref,
                  xn_ref, o_ref):
    xn = _rms(x_ref[...], g_ref[...]).astype(BF16)
    xn_ref[...] = xn
    acc = jnp.dot(xn, w_ref[...], preferred_element_type=F32)
    cos, slo, shi = cos_ref[...], slo_ref[...], shi_ref[...]
    for h in range(GQA_Q_HEADS + GQA_KV_HEADS):
        sl = slice(h * HEAD_DIM, (h + 1) * HEAD_DIM)
        if h < GQA_Q_HEADS:
            a = _rope(_rms(acc[:, sl], gq_ref[...]), cos, slo, shi) * Q_SCALE
        else:
            a = _rope(_rms(acc[:, sl], gk_ref[...]), cos, slo, shi)
        o_ref[:, sl] = a.astype(BF16)
    v0 = GQA_Q_WIDTH + GQA_KV_WIDTH
    o_ref[:, v0:] = acc[:, v0:].astype(BF16)


def _qkv_b(x2d, g_pre, w_b, cos, sin_lo, sin_hi, g_q, g_k, seq):
    t, d = x2d.shape
    tm = 512
    n_pos_blocks = seq // tm
    pos_spec = pl.BlockSpec((tm, HEAD_DIM), lambda i: (i % n_pos_blocks, 0))
    return pl.pallas_call(
        _qkv_b_kernel,
        out_shape=(jax.ShapeDtypeStruct((t, d), BF16), jax.ShapeDtypeStruct((t, N_ATTN_B), BF16)),
        grid=(t // tm,),
        in_specs=[
            pl.BlockSpec((tm, d), lambda i: (i, 0)),
            _resident((1, d)),
            _resident(w_b.shape),
            pos_spec, pos_spec, pos_spec,
            _resident((1, HEAD_DIM)),
            _resident((1, HEAD_DIM)),
        ],
        out_specs=(pl.BlockSpec((tm, d), lambda i: (i, 0)),
                   pl.BlockSpec((tm, N_ATTN_B), lambda i: (i, 0))),
        compiler_params=pltpu.CompilerParams(
            dimension_semantics=("parallel",),
            vmem_limit_bytes=VMEM_LIMIT),
        name="qkv_b",
    )(x2d, g_pre, w_b, cos, sin_lo, sin_hi, g_q, g_k)


def _gates_kernel(xn_ref, w_ref, o_ref):
    acc = jnp.dot(xn_ref[...], w_ref[...], preferred_element_type=F32)
    o_ref[...] = _sigmoid(acc).astype(BF16)


def _colscale_kernel(xn_ref, w_ref, cs_ref, o_ref):
    acc = jnp.dot(xn_ref[...], w_ref[...], preferred_element_type=F32)
    o_ref[...] = (acc * cs_ref[...]).astype(BF16)


def _xn_proj(body, name, xn, w, *extra):
    t, d = xn.shape
    n = w.shape[1]
    tm, tn = 2048, 1024
    extra_specs = [pl.BlockSpec((1, tn), lambda i, j: (0, j)) for _ in extra]
    return pl.pallas_call(
        body,
        out_shape=jax.ShapeDtypeStruct((t, n), BF16),
        grid=(t // tm, n // tn),
        in_specs=[
            pl.BlockSpec((tm, d), lambda i, j: (i, 0)),
            pl.BlockSpec((d, tn), lambda i, j: (0, j)),
        ] + extra_specs,
        out_specs=pl.BlockSpec((tm, tn), lambda i, j: (i, j)),
        compiler_params=pltpu.CompilerParams(
            dimension_semantics=("parallel", "arbitrary"),
            vmem_limit_bytes=VMEM_LIMIT),
        name=name,
    )(xn, w, *extra)


def _na_window_start(blk, rows):
    return jnp.clip(blk * NA_QROWS - NA_ROWS // 2, 0, rows - NA_WIN_ROWS)


def _na_kernel(q_ref, k_ref, v_ref, tab_ref, o_ref, *, rows):
    blk = pl.program_id(1)
    k0 = pl.multiple_of(_na_window_start(blk, rows) * GRID_W, GRID_W)
    for h in range(NA_HEADS):
        sl = slice(h * HEAD_DIM, (h + 1) * HEAD_DIM)
        q = q_ref[:, sl]
        k = k_ref[pl.ds(k0, NA_TK), sl]
        v = v_ref[pl.ds(k0, NA_TK), sl]
        s = lax.dot_general(q, k, (((1,), (1,)), ((), ())), preferred_element_type=F32)
        s = s + tab_ref[0, h]
        m = jnp.max(s, axis=-1, keepdims=True)
        p = jnp.exp2(s - m)
        l = jnp.sum(p, axis=-1, keepdims=True)
        o = jnp.dot(p.astype(BF16), v, preferred_element_type=F32)
        o_ref[:, sl] = (o * (1.0 / l)).astype(BF16)


def _na_bias_tables(rpb, rows):
    n_blk = rows // NA_QROWS
    n_dr = 2 * NA_ROWS - 1
    qc = np.arange(GRID_W)[:, None]
    kc = np.arange(GRID_W)[None, :]
    dc = np.clip(kc - qc, -(NA_COLS - 1), NA_COLS - 1) + NA_COLS - 1
    onehot = (dc.reshape(-1)[None, :] == np.arange(2 * NA_COLS - 1)[:, None]).astype(np.float32)
    cexp = jnp.einsum("hrd,dx->hrx", rpb.astype(F32) * LOG2E, jnp.asarray(onehot),
                      precision=lax.Precision.HIGHEST)
    cexp = cexp.reshape(NA_HEADS, n_dr, GRID_W, GRID_W)
    pad = NA_WIN_ROWS
    cexp = jnp.pad(cexp, ((0, 0), (pad, pad), (0, 0), (0, 0)))
    cs = np.clip(qc - NA_COLS // 2, 0, GRID_W - NA_COLS)
    col_ok = (kc >= cs) & (kc < cs + NA_COLS)
    tabs = []
    for blk in (0, 1, n_blk - 1):
        r0 = blk * NA_QROWS
        k0 = int(np.clip(r0 - NA_ROWS // 2, 0, rows - NA_WIN_ROWS))
        per_row = []
        for qr in range(r0, r0 + NA_QROWS):
            start = k0 - qr + NA_ROWS - 1 + pad
            per_row.append(cexp[:, start:start + NA_WIN_ROWS])
        b = jnp.stack(per_row, axis=1)
        qr_ = (r0 + np.arange(NA_QROWS))[:, None]
        kr_ = (k0 + np.arange(NA_WIN_ROWS))[None, :]
        rs = np.clip(qr_ - NA_ROWS // 2, 0, rows - NA_ROWS)
        row_ok = (kr_ >= rs) & (kr_ < rs + NA_ROWS)
        ok = row_ok[:, :, None, None] & col_ok[None, None, :, :]
        b = jnp.where(jnp.asarray(ok)[None], b, NEG_INF)
        b = b.transpose(0, 1, 3, 2, 4).reshape(NA_HEADS, NA_TQ, NA_TK)
        tabs.append(b)
    return jnp.stack(tabs)


def _na_attention(pa, tables, batch, seq):
    rows = seq // GRID_W
    n_blk = rows // NA_QROWS
    t = batch * seq

    def tab_index(b, blk):
        return (jnp.where(blk == 0, 0, jnp.where(blk == n_blk - 1, 2, 1)), 0, 0, 0)

    return pl.pallas_call(
        functools.partial(_na_kernel, rows=rows),
        out_shape=jax.ShapeDtypeStruct((t, NA_WIDTH), BF16),
        grid=(batch, n_blk),
        in_specs=[
            pl.BlockSpec((NA_TQ, NA_WIDTH), lambda b, blk: (b * n_blk + blk, 0)),
            pl.BlockSpec((seq, NA_WIDTH), lambda b, blk: (b, 1)),
            pl.BlockSpec((seq, NA_WIDTH), lambda b, blk: (b, 2)),
            pl.BlockSpec((1, NA_HEADS, NA_TQ, NA_TK), tab_index),
        ],
        out_specs=pl.BlockSpec((NA_TQ, NA_WIDTH), lambda b, blk: (b * n_blk + blk, 0)),
        compiler_params=pltpu.CompilerParams(
            dimension_semantics=("parallel", "arbitrary"),
            vmem_limit_bytes=VMEM_LIMIT),
        name="na_attn",
    )(pa, pa, pa, tables)


GQA_TQ = 256


def _gqa_kernel(q_ref, k_ref, v_ref, o_ref, vaug_ref):
    @pl.when(pl.program_id(2) == 0)
    def _():
        vaug_ref[:, :HEAD_DIM] = v_ref[...]
        vaug_ref[:, HEAD_DIM:] = jnp.ones((v_ref.shape[0], HEAD_DIM), BF16)

    k = k_ref[...]
    for h in range(GQA_GROUP):
        sl = slice(h * HEAD_DIM, (h + 1) * HEAD_DIM)
        s = lax.dot_general(q_ref[:, sl], k, (((1,), (1,)), ((), ())), preferred_element_type=F32)
        m = jnp.max(s, axis=-1, keepdims=True)
        p = jnp.exp2(s - m).astype(BF16)
        ol = jnp.dot(p, vaug_ref[...], preferred_element_type=F32)
        o_ref[:, sl] = (ol[:, :HEAD_DIM] * (1.0 / ol[:, HEAD_DIM:])).astype(BF16)


def _gqa_attention(pb, batch, seq):
    t = batch * seq
    n_q = seq // GQA_TQ
    gw = GQA_GROUP * HEAD_DIM
    return pl.pallas_call(
        _gqa_kernel,
        out_shape=jax.ShapeDtypeStruct((t, GQA_Q_WIDTH), BF16),
        grid=(batch, GQA_KV_HEADS, n_q),
        in_specs=[
            pl.BlockSpec((GQA_TQ, gw), lambda b, g, i: (b * n_q + i, g)),
            pl.BlockSpec((seq, HEAD_DIM), lambda b, g, i: (b, GQA_Q_WIDTH // HEAD_DIM + g)),
            pl.BlockSpec((seq, HEAD_DIM),
                         lambda b, g, i: (b, (GQA_Q_WIDTH + GQA_KV_WIDTH) // HEAD_DIM + g)),
        ],
        out_specs=pl.BlockSpec((GQA_TQ, gw), lambda b, g, i: (b * n_q + i, g)),
        scratch_shapes=[pltpu.VMEM((seq, 2 * HEAD_DIM), BF16)],
        compiler_params=pltpu.CompilerParams(
            dimension_semantics=("parallel", "parallel", "arbitrary"),
            vmem_limit_bytes=VMEM_LIMIT),
        name="gqa_attn",
    )(pb, pb, pb)


def _mix_kernel(a_ref, b_ref, ga_ref, gb_ref, x_ref, wa_ref, wb_ref, wo_ref,
                gpost_ref, gpre_ref, h_ref, xn_ref):
    ya = jnp.dot(a_ref[...], wa_ref[...], preferred_element_type=F32)
    yb = jnp.dot(b_ref[...], wb_ref[...], preferred_element_type=F32)
    mix = ga_ref[...].astype(F32) * ya + gb_ref[...].astype(F32) * yb
    m2 = jnp.dot(mix.astype(BF16), wo_ref[...], preferred_element_type=F32)
    h = x_ref[...] + _rms(m2, gpost_ref[...])
    h_ref[...] = h
    xn_ref[...] = _rms(h, gpre_ref[...]).astype(BF16)


def _mix(na_out, gqa_out, gates, x2d, wa, wb, wo, g_post, g_pre_ffn):
    t, d = x2d.shape
    tm = 256
    return pl.pallas_call(
        _mix_kernel,
        out_shape=(jax.ShapeDtypeStruct((t, d), F32), jax.ShapeDtypeStruct((t, d), BF16)),
        grid=(t // tm,),
        in_specs=[
            pl.BlockSpec((tm, NA_WIDTH), lambda i: (i, 0)),
            pl.BlockSpec((tm, GQA_Q_WIDTH), lambda i: (i, 0)),
            pl.BlockSpec((tm, d), lambda i: (i, 0)),
            pl.BlockSpec((tm, d), lambda i: (i, 1)),
            pl.BlockSpec((tm, d), lambda i: (i, 0)),
            _resident(wa.shape),
            _resident(wb.shape),
            _resident(wo.shape),
            _resident((1, d)),
            _resident((1, d)),
        ],
        out_specs=(pl.BlockSpec((tm, d), lambda i: (i, 0)), pl.BlockSpec((tm, d), lambda i: (i, 0))),
        compiler_params=pltpu.CompilerParams(
            dimension_semantics=("parallel",),
            vmem_limit_bytes=VMEM_LIMIT),
        name="mix",
    )(na_out, gqa_out, gates, gates, x2d, wa, wb, wo, g_post, g_pre_ffn)


FFN_TF = 512


def _ffn_kernel(xn_ref, wg_ref, wu_ref, wo_ref, f_ref, acc_ref):
    j = pl.program_id(1)
    xn = xn_ref[...]
    g = jnp.dot(xn, wg_ref[...], preferred_element_type=F32)
    u = jnp.dot(xn, wu_ref[...], preferred_element_type=F32)
    act = (g * _sigmoid(g) * u).astype(BF16)
    part = jnp.dot(act, wo_ref[...], preferred_element_type=F32)

    @pl.when(j == 0)
    def _():
        acc_ref[...] = part

    @pl.when(j > 0)
    def _():
        acc_ref[...] += part

    @pl.when(j == pl.num_programs(1) - 1)
    def _():
        f_ref[...] = acc_ref[...].astype(BF16)


def _ffn(xn2, w_ffn_in, w_ffn_out):
    t, d = xn2.shape
    d_ff = w_ffn_out.shape[0]
    tm = 1024
    n_f = d_ff // FFN_TF
    return pl.pallas_call(
        _ffn_kernel,
        out_shape=jax.ShapeDtypeStruct((t, d), BF16),
        grid=(t // tm, n_f),
        in_specs=[
            pl.BlockSpec((tm, d), lambda i, j: (i, 0)),
            pl.BlockSpec((d, FFN_TF), lambda i, j: (0, j)),
            pl.BlockSpec((d, FFN_TF), lambda i, j: (0, n_f + j)),
            pl.BlockSpec((FFN_TF, d), lambda i, j: (j, 0)),
        ],
        out_specs=pl.BlockSpec((tm, d), lambda i, j: (i, 0)),
        scratch_shapes=[pltpu.VMEM((tm, d), F32)],
        compiler_params=pltpu.CompilerParams(
            dimension_semantics=("parallel", "arbitrary"),
            vmem_limit_bytes=VMEM_LIMIT),
        name="ffn",
    )(xn2, w_ffn_in, w_ffn_in, w_ffn_out)


def _ple_kernel(h_ref, f_ref, p_ref, wple_ref, wgate_ref, gffn_ref, gple_ref, y_ref):
    h = h_ref[...] + _rms(f_ref[...].astype(F32), gffn_ref[...])
    e = jnp.dot(p_ref[...].astype(BF16), wple_ref[...], preferred_element_type=F32)
    gate = _sigmoid(jnp.dot(h.astype(BF16), wgate_ref[...], preferred_element_type=F32))
    y_ref[...] = h + _rms(gate * e, gple_ref[...])


def _ple(h1, f, p2d, w_ple, w_gate, g_ffn, g_ple):
    t, d = h1.shape
    pd = p2d.shape[1]
    tm = 256
    return pl.pallas_call(
        _ple_kernel,
        out_shape=jax.ShapeDtypeStruct((t, d), F32),
        grid=(t // tm,),
        in_specs=[
            pl.BlockSpec((tm, d), lambda i: (i, 0)),
            pl.BlockSpec((tm, d), lambda i: (i, 0)),
            pl.BlockSpec((tm, pd), lambda i: (i, 0)),
            _resident(w_ple.shape),
            _resident(w_gate.shape),
            _resident((1, d)),
            _resident((1, d)),
        ],
        out_specs=pl.BlockSpec((tm, d), lambda i: (i, 0)),
        compiler_params=pltpu.CompilerParams(
            dimension_semantics=("parallel",),
            vmem_limit_bytes=VMEM_LIMIT),
        name="ple",
    )(h1, f, p2d, w_ple, w_gate, g_ffn, g_ple)


def _rope_tables(seq):
    half = HEAD_DIM // 2
    t = jnp.arange(seq)
    inv_freq = ROPE_THETA ** (-jnp.arange(0, half, 2, dtype=F32) / half)

    def tables(pos):
        ang = pos.astype(F32)[:, None] * inv_freq[None, :]
        return (jnp.concatenate([jnp.cos(ang), jnp.cos(ang)], axis=-1),
                jnp.concatenate([jnp.sin(ang), jnp.sin(ang)], axis=-1))

    cos_r, sin_r = tables(t // GRID_W)
    cos_c, sin_c = tables(t % GRID_W)
    cos = jnp.concatenate([cos_r, cos_c], axis=-1)
    sin = jnp.concatenate([sin_r, sin_c], axis=-1)
    first = (jnp.arange(HEAD_DIM) % half) < half // 2
    sin_lo = jnp.where(first[None, :], -sin, 0.0)
    sin_hi = jnp.where(first[None, :], 0.0, sin)
    return cos, sin_lo, sin_hi


def _layer(x, p, w, seq):
    batch = x.shape[0]
    d = x.shape[-1]
    x2d = x.reshape(batch * seq, d)
    p2d = p.reshape(batch * seq, p.shape[-1])
    xn, pb = _qkv_b(x2d, w["g_pre_mix"], w["w_qkv_b"], w["cos"], w["sin_lo"], w["sin_hi"],
                    w["g_q"], w["g_k"], seq)
    gates = _xn_proj(_gates_kernel, "gates", xn, w["w_gates"])
    pa = _xn_proj(_colscale_kernel, "qkv_a", xn, w["w_qkv_a"], w["qkv_a_scale"])
    na_out = _na_attention(pa, w["na_tables"], batch, seq)
    gqa_out = _gqa_attention(pb, batch, seq)
    h1, xn2 = _mix(na_out, gqa_out, gates, x2d, w["w_branch_a"], w["w_branch_b"], w["w_out"],
                   w["g_post_mix"], w["g_pre_ffn"])
    f = _ffn(xn2, w["w_ffn_in"], w["w_ffn_out"])
    y = _ple(h1, f, p2d, w["w_ple"], w["w_ple_gate"], w["g_post_ffn"], w["g_ple"])
    return y.reshape(x.shape)


def kernel(x_prompt, x_sample, p_prompt, p_sample, g_pre_mix, w_in, rpb, g_q, g_k, w_branch_a, w_branch_b, w_out, g_post_mix, g_pre_ffn, w_ffn_in, w_ffn_out, g_post_ffn, w_ple, w_ple_gate, g_ple):
    depth = w_in.shape[0]
    h_p, h_s = x_prompt, x_sample
    qkv_a_scale = jnp.concatenate(
        [jnp.full((1, NA_WIDTH), Q_SCALE, F32), jnp.ones((1, 2 * NA_WIDTH), F32)], axis=1)
    for i in range(depth):
        w = {
            "w_qkv_a": w_in[i][:, :N_ATTN_A].astype(BF16),
            "w_qkv_b": w_in[i][:, N_ATTN_A:N_ATTN_A + N_ATTN_B].astype(BF16),
            "w_gates": w_in[i][:, N_ATTN_A + N_ATTN_B:].astype(BF16),
            "qkv_a_scale": qkv_a_scale,
            "w_branch_a": w_branch_a[i].astype(BF16),
            "w_branch_b": w_branch_b[i].astype(BF16),
            "w_out": w_out[i].astype(BF16),
            "w_ffn_in": w_ffn_in[i].astype(BF16),
            "w_ffn_out": w_ffn_out[i].astype(BF16),
            "w_ple": w_ple[i].astype(BF16),
            "w_ple_gate": w_ple_gate[i].astype(BF16),
            "g_pre_mix": g_pre_mix[i][None, :],
            "g_q": g_q[i][None, :],
            "g_k": g_k[i][None, :],
            "g_post_mix": g_post_mix[i][None, :],
            "g_pre_ffn": g_pre_ffn[i][None, :],
            "g_post_ffn": g_post_ffn[i][None, :],
            "g_ple": g_ple[i][None, :],
        }
        outs = []
        for h, p in ((h_p, p_prompt[i]), (h_s, p_sample[i])):
            seq = h.shape[1]
            w["cos"], w["sin_lo"], w["sin_hi"] = _rope_tables(seq)
            w["na_tables"] = _na_bias_tables(rpb[i], seq // GRID_W)
            outs.append(_layer(h, p, w, seq))
        h_p, h_s = outs
    return (h_p, h_s)
```

```python
import functools
import math

import numpy as np
import jax
import jax.numpy as jnp
from jax import lax
from jax.experimental import pallas as pl
from jax.experimental.pallas import tpu as pltpu

F32 = jnp.float32
BF16 = jnp.bfloat16

HEAD_DIM = 128
NA_HEADS = 8
GQA_Q_HEADS = 8
GQA_KV_HEADS = 2
GQA_GROUP = GQA_Q_HEADS // GQA_KV_HEADS
GRID_W = 64
NA_ROWS = 8
NA_COLS = 16
ROPE_THETA = 10000.0
RMS_EPS = 1e-6
NEG_INF = -1e30
LOG2E = math.log2(math.e)
Q_SCALE = LOG2E / math.sqrt(HEAD_DIM)

NA_WIDTH = NA_HEADS * HEAD_DIM
GQA_Q_WIDTH = GQA_Q_HEADS * HEAD_DIM
GQA_KV_WIDTH = GQA_KV_HEADS * HEAD_DIM
N_ATTN_A = 3 * NA_WIDTH
N_ATTN_B = GQA_Q_WIDTH + 2 * GQA_KV_WIDTH

NA_QROWS = 4
NA_WIN_ROWS = 12
NA_TQ = NA_QROWS * GRID_W
NA_TK = NA_WIN_ROWS * GRID_W

VMEM_LIMIT = 56 * 1024 * 1024


def _rms(x, g):
    ms = jnp.mean(x * x, axis=-1, keepdims=True)
    return x * lax.rsqrt(ms + RMS_EPS) * g


def _sigmoid(x):
    return 0.5 * jnp.tanh(0.5 * x) + 0.5


def _resident(shape):
    return pl.BlockSpec(shape, lambda *_: (0,) * len(shape), pipeline_mode=pl.Buffered(1))


def _rope(a, cos, sin_lo, sin_hi):
    return a * cos + pltpu.roll(a, 96, 1) * sin_lo + pltpu.roll(a, 32, 1) * sin_hi


CHAIN_ROWS = 256


def _row_chunks(n_rows):
    return [slice(r, r + CHAIN_ROWS) for r in range(0, n_rows, CHAIN_ROWS)]


def _qkv_b_kernel(x_ref, g_ref, w_ref, cos_ref, slo_ref, shi_ref, gq_ref, gk_ref,
                  xn_ref, o_ref):
    for rows in _row_chunks(x_ref.shape[0]):
        xn = _rms(x_ref[rows, :], g_ref[...]).astype(BF16)
        xn_ref[rows, :] = xn
        acc = jnp.dot(xn, w_ref[...], preferred_element_type=F32)
        cos, slo, shi = cos_ref[rows, :], slo_ref[rows, :], shi_ref[rows, :]
        for h in range(GQA_Q_HEADS + GQA_KV_HEADS):
            sl = slice(h * HEAD_DIM, (h + 1) * HEAD_DIM)
            if h < GQA_Q_HEADS:
                a = _rope(_rms(acc[:, sl], gq_ref[...]), cos, slo, shi) * Q_SCALE
            else:
                a = _rope(_rms(acc[:, sl], gk_ref[...]), cos, slo, shi)
            o_ref[rows, sl] = a.astype(BF16)
        v0 = GQA_Q_WIDTH + GQA_KV_WIDTH
        o_ref[rows, v0:] = acc[:, v0:].astype(BF16)


def _qkv_b(x2d, g_pre, w_b, cos, sin_lo, sin_hi, g_q, g_k, seq):
    t, d = x2d.shape
    tm = 512
    n_pos_blocks = seq // tm
    pos_spec = pl.BlockSpec((tm, HEAD_DIM), lambda i: (i % n_pos_blocks, 0))
    return pl.pallas_call(
        _qkv_b_kernel,
        out_shape=(jax.ShapeDtypeStruct((t, d), BF16), jax.ShapeDtypeStruct((t, N_ATTN_B), BF16)),
        grid=(t // tm,),
        in_specs=[
            pl.BlockSpec((tm, d), lambda i: (i, 0)),
            _resident((1, d)),
            _resident(w_b.shape),
            pos_spec, pos_spec, pos_spec,
            _resident((1, HEAD_DIM)),
            _resident((1, HEAD_DIM)),
        ],
        out_specs=(pl.BlockSpec((tm, d), lambda i: (i, 0)),
                   pl.BlockSpec((tm, N_ATTN_B), lambda i: (i, 0))),
        compiler_params=pltpu.CompilerParams(
            dimension_semantics=("parallel",),
            vmem_limit_bytes=VMEM_LIMIT),
        name="qkv_b",
    )(x2d, g_pre, w_b, cos, sin_lo, sin_hi, g_q, g_k)


def _colscale_kernel(xn_ref, w_ref, cs_ref, o_ref):
    acc = jnp.dot(xn_ref[...], w_ref[...], preferred_element_type=F32)
    o_ref[...] = (acc * cs_ref[...]).astype(BF16)


def _xn_proj(body, name, xn, w, *extra):
    t, d = xn.shape
    n = w.shape[1]
    tm, tn = 2048, 1024
    extra_specs = [pl.BlockSpec((1, tn), lambda i, j: (0, j)) for _ in extra]
    return pl.pallas_call(
        body,
        out_shape=jax.ShapeDtypeStruct((t, n), BF16),
        grid=(t // tm, n // tn),
        in_specs=[
            pl.BlockSpec((tm, d), lambda i, j: (i, 0)),
            pl.BlockSpec((d, tn), lambda i, j: (0, j)),
        ] + extra_specs,
        out_specs=pl.BlockSpec((tm, tn), lambda i, j: (i, j)),
        compiler_params=pltpu.CompilerParams(
            dimension_semantics=("parallel", "arbitrary"),
            vmem_limit_bytes=VMEM_LIMIT),
        name=name,
    )(xn, w, *extra)


def _na_window_start(blk, rows):
    return jnp.clip(blk * NA_QROWS - NA_ROWS // 2, 0, rows - NA_WIN_ROWS)


def _na_kernel(q_ref, k_ref, v_ref, tab_ref, o_ref, *, rows):
    blk = pl.program_id(1)
    n_blk = pl.num_programs(1)
    k0 = pl.multiple_of(_na_window_start(blk, rows) * GRID_W, GRID_W)
    variant = jnp.where(blk == 0, 0, jnp.where(blk == n_blk - 1, 2, 1))
    ones = jnp.ones((NA_TK, HEAD_DIM), BF16)
    for h in range(NA_HEADS):
        sl = slice(h * HEAD_DIM, (h + 1) * HEAD_DIM)
        q = q_ref[:, sl]
        k = k_ref[pl.ds(k0, NA_TK), sl]
        v = v_ref[pl.ds(k0, NA_TK), sl]
        s = lax.dot_general(q, k, (((1,), (1,)), ((), ())), preferred_element_type=F32)
        s = s + tab_ref[variant, h]
        m = jnp.max(s, axis=-1, keepdims=True)
        p = jnp.exp2(s - m).astype(BF16)
        ol = jnp.dot(p, jnp.concatenate([v, ones], axis=1), preferred_element_type=F32)
        o_ref[:, sl] = (ol[:, :HEAD_DIM] * (1.0 / ol[:, HEAD_DIM:])).astype(BF16)


def _na_bias_tables(rpb, rows):
    n_blk = rows // NA_QROWS
    n_dr = 2 * NA_ROWS - 1
    qc = np.arange(GRID_W)[:, None]
    kc = np.arange(GRID_W)[None, :]
    dc = np.clip(kc - qc, -(NA_COLS - 1), NA_COLS - 1) + NA_COLS - 1
    onehot = (dc.reshape(-1)[None, :] == np.arange(2 * NA_COLS - 1)[:, None]).astype(np.float32)
    cexp = jnp.einsum("hrd,dx->hrx", rpb.astype(F32) * LOG2E, jnp.asarray(onehot),
                      precision=lax.Precision.HIGHEST)
    cexp = cexp.reshape(NA_HEADS, n_dr, GRID_W, GRID_W)
    pad = NA_WIN_ROWS
    cexp = jnp.pad(cexp, ((0, 0), (pad, pad), (0, 0), (0, 0)))
    cs = np.clip(qc - NA_COLS // 2, 0, GRID_W - NA_COLS)
    col_ok = (kc >= cs) & (kc < cs + NA_COLS)
    tabs = []
    for blk in (0, 1, n_blk - 1):
        r0 = blk * NA_QROWS
        k0 = int(np.clip(r0 - NA_ROWS // 2, 0, rows - NA_WIN_ROWS))
        per_row = []
        for qr in range(r0, r0 + NA_QROWS):
            start = k0 - qr + NA_ROWS - 1 + pad
            per_row.append(cexp[:, start:start + NA_WIN_ROWS])
        b = jnp.stack(per_row, axis=1)
        qr_ = (r0 + np.arange(NA_QROWS))[:, None]
        kr_ = (k0 + np.arange(NA_WIN_ROWS))[None, :]
        rs = np.clip(qr_ - NA_ROWS // 2, 0, rows - NA_ROWS)
        row_ok = (kr_ >= rs) & (kr_ < rs + NA_ROWS)
        ok = row_ok[:, :, None, None] & col_ok[None, None, :, :]
        b = jnp.where(jnp.asarray(ok)[None], b, NEG_INF)
        b = b.transpose(0, 1, 3, 2, 4).reshape(NA_HEADS, NA_TQ, NA_TK)
        tabs.append(b)
    return jnp.stack(tabs)


def _na_attention(pa, tables, batch, seq):
    rows = seq // GRID_W
    n_blk = rows // NA_QROWS
    t = batch * seq

    return pl.pallas_call(
        functools.partial(_na_kernel, rows=rows),
        out_shape=jax.ShapeDtypeStruct((t, NA_WIDTH), BF16),
        grid=(batch, n_blk),
        in_specs=[
            pl.BlockSpec((NA_TQ, NA_WIDTH), lambda b, blk: (b * n_blk + blk, 0)),
            pl.BlockSpec((seq, NA_WIDTH), lambda b, blk: (b, 1)),
            pl.BlockSpec((seq, NA_WIDTH), lambda b, blk: (b, 2)),
            _resident(tables.shape),
        ],
        out_specs=pl.BlockSpec((NA_TQ, NA_WIDTH), lambda b, blk: (b * n_blk + blk, 0)),
        compiler_params=pltpu.CompilerParams(
            dimension_semantics=("parallel", "arbitrary"),
            vmem_limit_bytes=VMEM_LIMIT),
        name="na_attn",
    )(pa, pa, pa, tables)


GQA_TQ = 512
GQA_CHAIN_ROWS = 256


def _gqa_kernel(q_ref, k_ref, v_ref, o_ref, vaug_ref):
    @pl.when(pl.program_id(2) == 0)
    def _():
        vaug_ref[:, :HEAD_DIM] = v_ref[...]
        vaug_ref[:, HEAD_DIM:] = jnp.ones((v_ref.shape[0], HEAD_DIM), BF16)

    k = k_ref[...]
    for r in range(GQA_TQ // GQA_CHAIN_ROWS):
        rows = slice(r * GQA_CHAIN_ROWS, (r + 1) * GQA_CHAIN_ROWS)
        for h in range(GQA_GROUP):
            sl = slice(h * HEAD_DIM, (h + 1) * HEAD_DIM)
            s = lax.dot_general(q_ref[rows, sl], k, (((1,), (1,)), ((), ())),
                                preferred_element_type=F32)
            m = jnp.max(s, axis=-1, keepdims=True)
            p = jnp.exp2(s - m).astype(BF16)
            ol = jnp.dot(p, vaug_ref[...], preferred_element_type=F32)
            o_ref[rows, sl] = (ol[:, :HEAD_DIM] * (1.0 / ol[:, HEAD_DIM:])).astype(BF16)


def _gqa_attention(pb, batch, seq):
    t = batch * seq
    n_q = seq // GQA_TQ
    gw = GQA_GROUP * HEAD_DIM
    return pl.pallas_call(
        _gqa_kernel,
        out_shape=jax.ShapeDtypeStruct((t, GQA_Q_WIDTH), BF16),
        grid=(batch, GQA_KV_HEADS, n_q),
        in_specs=[
            pl.BlockSpec((GQA_TQ, gw), lambda b, g, i: (b * n_q + i, g)),
            pl.BlockSpec((seq, HEAD_DIM), lambda b, g, i: (b, GQA_Q_WIDTH // HEAD_DIM + g)),
            pl.BlockSpec((seq, HEAD_DIM),
                         lambda b, g, i: (b, (GQA_Q_WIDTH + GQA_KV_WIDTH) // HEAD_DIM + g)),
        ],
        out_specs=pl.BlockSpec((GQA_TQ, gw), lambda b, g, i: (b * n_q + i, g)),
        scratch_shapes=[pltpu.VMEM((seq, 2 * HEAD_DIM), BF16)],
        compiler_params=pltpu.CompilerParams(
            dimension_semantics=("parallel", "parallel", "arbitrary"),
            vmem_limit_bytes=VMEM_LIMIT),
        name="gqa_attn",
    )(pb, pb, pb)


def _mix1_kernel(xn_ref, a_ref, b_ref, wga_ref, wgb_ref, wa_ref, wb_ref, o_ref):
    xn = xn_ref[...]
    ga = _sigmoid(jnp.dot(xn, wga_ref[...], preferred_element_type=F32))
    ya = jnp.dot(a_ref[...], wa_ref[...], preferred_element_type=F32)
    gb = _sigmoid(jnp.dot(xn, wgb_ref[...], preferred_element_type=F32))
    yb = jnp.dot(b_ref[...], wb_ref[...], preferred_element_type=F32)
    o_ref[...] = (ga * ya + gb * yb).astype(BF16)


def _mix1(xn, na_out, gqa_out, w_gates, wa, wb):
    t, d = xn.shape
    tm, tn = 1024, 512
    n_j = d // tn
    return pl.pallas_call(
        _mix1_kernel,
        out_shape=jax.ShapeDtypeStruct((t, d), BF16),
        grid=(t // tm, n_j),
        in_specs=[
            pl.BlockSpec((tm, d), lambda i, j: (i, 0)),
            pl.BlockSpec((tm, NA_WIDTH), lambda i, j: (i, 0)),
            pl.BlockSpec((tm, GQA_Q_WIDTH), lambda i, j: (i, 0)),
            pl.BlockSpec((d, tn), lambda i, j: (0, j)),
            pl.BlockSpec((d, tn), lambda i, j: (0, n_j + j)),
            pl.BlockSpec((NA_WIDTH, tn), lambda i, j: (0, j)),
            pl.BlockSpec((GQA_Q_WIDTH, tn), lambda i, j: (0, j)),
        ],
        out_specs=pl.BlockSpec((tm, tn), lambda i, j: (i, j)),
        compiler_params=pltpu.CompilerParams(
            dimension_semantics=("parallel", "arbitrary"),
            vmem_limit_bytes=VMEM_LIMIT),
        name="mix1",
    )(xn, na_out, gqa_out, w_gates, w_gates, wa, wb)


def _mix2_kernel(m_ref, x_ref, wo_ref, gpost_ref, gpre_ref, h_ref, xn_ref):
    for rows in _row_chunks(m_ref.shape[0]):
        m2 = jnp.dot(m_ref[rows, :], wo_ref[...], preferred_element_type=F32)
        h = x_ref[rows, :] + _rms(m2, gpost_ref[...])
        h_ref[rows, :] = h
        xn_ref[rows, :] = _rms(h, gpre_ref[...]).astype(BF16)


def _mix2(mixed, x2d, wo, g_post, g_pre_ffn):
    t, d = x2d.shape
    tm = 512
    row_spec = pl.BlockSpec((tm, d), lambda i: (i, 0))
    return pl.pallas_call(
        _mix2_kernel,
        out_shape=(jax.ShapeDtypeStruct((t, d), F32), jax.ShapeDtypeStruct((t, d), BF16)),
        grid=(t // tm,),
        in_specs=[row_spec, row_spec, _resident(wo.shape), _resident((1, d)), _resident((1, d))],
        out_specs=(row_spec, row_spec),
        compiler_params=pltpu.CompilerParams(
            dimension_semantics=("parallel",),
            vmem_limit_bytes=VMEM_LIMIT),
        name="mix2",
    )(mixed, x2d, wo, g_post, g_pre_ffn)


FFN_TF = 512


def _ffn_kernel(xn_ref, wg_ref, wu_ref, wo_ref, f_ref, acc_ref):
    j = pl.program_id(1)

    @pl.when(j == 0)
    def _():
        acc_ref[...] = jnp.zeros_like(acc_ref)

    xn = xn_ref[...]
    g = jnp.dot(xn, wg_ref[...], preferred_element_type=F32)
    u = jnp.dot(xn, wu_ref[...], preferred_element_type=F32)
    act = (g * _sigmoid(g) * u).astype(BF16)
    acc_ref[...] += jnp.dot(act, wo_ref[...], preferred_element_type=F32)

    @pl.when(j == pl.num_programs(1) - 1)
    def _():
        f_ref[...] = acc_ref[...].astype(BF16)


def _ffn(xn2, w_ffn_in, w_ffn_out):
    t, d = xn2.shape
    d_ff = w_ffn_out.shape[0]
    tm = 1024
    n_f = d_ff // FFN_TF
    return pl.pallas_call(
        _ffn_kernel,
        out_shape=jax.ShapeDtypeStruct((t, d), BF16),
        grid=(t // tm, n_f),
        in_specs=[
            pl.BlockSpec((tm, d), lambda i, j: (i, 0)),
            pl.BlockSpec((d, FFN_TF), lambda i, j: (0, j)),
            pl.BlockSpec((d, FFN_TF), lambda i, j: (0, n_f + j)),
            pl.BlockSpec((FFN_TF, d), lambda i, j: (j, 0)),
        ],
        out_specs=pl.BlockSpec((tm, d), lambda i, j: (i, 0)),
        scratch_shapes=[pltpu.VMEM((tm, d), F32)],
        compiler_params=pltpu.CompilerParams(
            dimension_semantics=("parallel", "arbitrary"),
            vmem_limit_bytes=VMEM_LIMIT),
        name="ffn",
    )(xn2, w_ffn_in, w_ffn_in, w_ffn_out)


def _ple_kernel(h_ref, f_ref, p_ref, wple_ref, wgate_ref, gffn_ref, gple_ref, y_ref):
    for rows in _row_chunks(h_ref.shape[0]):
        h = h_ref[rows, :] + _rms(f_ref[rows, :].astype(F32), gffn_ref[...])
        e = jnp.dot(p_ref[rows, :].astype(BF16), wple_ref[...], preferred_element_type=F32)
        gate = _sigmoid(jnp.dot(h.astype(BF16), wgate_ref[...], preferred_element_type=F32))
        y_ref[rows, :] = h + _rms(gate * e, gple_ref[...])


def _ple(h1, f, p2d, w_ple, w_gate, g_ffn, g_ple):
    t, d = h1.shape
    pd = p2d.shape[1]
    tm = 512
    return pl.pallas_call(
        _ple_kernel,
        out_shape=jax.ShapeDtypeStruct((t, d), F32),
        grid=(t // tm,),
        in_specs=[
            pl.BlockSpec((tm, d), lambda i: (i, 0)),
            pl.BlockSpec((tm, d), lambda i: (i, 0)),
            pl.BlockSpec((tm, pd), lambda i: (i, 0)),
            _resident(w_ple.shape),
            _resident(w_gate.shape),
            _resident((1, d)),
            _resident((1, d)),
        ],
        out_specs=pl.BlockSpec((tm, d), lambda i: (i, 0)),
        compiler_params=pltpu.CompilerParams(
            dimension_semantics=("parallel",),
            vmem_limit_bytes=VMEM_LIMIT),
        name="ple",
    )(h1, f, p2d, w_ple, w_gate, g_ffn, g_ple)


def _rope_tables(seq):
    half = HEAD_DIM // 2
    t = jnp.arange(seq)
    inv_freq = ROPE_THETA ** (-jnp.arange(0, half, 2, dtype=F32) / half)

    def tables(pos):
        ang = pos.astype(F32)[:, None] * inv_freq[None, :]
        return (jnp.concatenate([jnp.cos(ang), jnp.cos(ang)], axis=-1),
                jnp.concatenate([jnp.sin(ang), jnp.sin(ang)], axis=-1))

    cos_r, sin_r = tables(t // GRID_W)
    cos_c, sin_c = tables(t % GRID_W)
    cos = jnp.concatenate([cos_r, cos_c], axis=-1)
    sin = jnp.concatenate([sin_r, sin_c], axis=-1)
    first = (jnp.arange(HEAD_DIM) % half) < half // 2
    sin_lo = jnp.where(first[None, :], -sin, 0.0)
    sin_hi = jnp.where(first[None, :], 0.0, sin)
    return cos, sin_lo, sin_hi


def _layer(x, p, w, seq):
    batch = x.shape[0]
    d = x.shape[-1]
    x2d = x.reshape(batch * seq, d)
    p2d = p.reshape(batch * seq, p.shape[-1])
    xn, pb = _qkv_b(x2d, w["g_pre_mix"], w["w_qkv_b"], w["cos"], w["sin_lo"], w["sin_hi"],
                    w["g_q"], w["g_k"], seq)
    pa = _xn_proj(_colscale_kernel, "qkv_a", xn, w["w_qkv_a"], w["qkv_a_scale"])
    na_out = _na_attention(pa, w["na_tables"], batch, seq)
    gqa_out = _gqa_attention(pb, batch, seq)
    mixed = _mix1(xn, na_out, gqa_out, w["w_gates"], w["w_branch_a"], w["w_branch_b"])
    h1, xn2 = _mix2(mixed, x2d, w["w_out"], w["g_post_mix"], w["g_pre_ffn"])
    f = _ffn(xn2, w["w_ffn_in"], w["w_ffn_out"])
    y = _ple(h1, f, p2d, w["w_ple"], w["w_ple_gate"], w["g_post_ffn"], w["g_ple"])
    return y.reshape(x.shape)


def kernel(x_prompt, x_sample, p_prompt, p_sample, g_pre_mix, w_in, rpb, g_q, g_k, w_branch_a, w_branch_b, w_out, g_post_mix, g_pre_ffn, w_ffn_in, w_ffn_out, g_post_ffn, w_ple, w_ple_gate, g_ple):
    depth = w_in.shape[0]
    h_p, h_s = x_prompt, x_sample
    qkv_a_scale = jnp.concatenate(
        [jnp.full((1, NA_WIDTH), Q_SCALE, F32), jnp.ones((1, 2 * NA_WIDTH), F32)], axis=1)
    for i in range(depth):
        w = {
            "w_qkv_a": w_in[i][:, :N_ATTN_A].astype(BF16),
            "w_qkv_b": w_in[i][:, N_ATTN_A:N_ATTN_A + N_ATTN_B].astype(BF16),
            "w_gates": w_in[i][:, N_ATTN_A + N_ATTN_B:].astype(BF16),
            "qkv_a_scale": qkv_a_scale,
            "w_branch_a": w_branch_a[i].astype(BF16),
            "w_branch_b": w_branch_b[i].astype(BF16),
            "w_out": w_out[i].astype(BF16),
            "w_ffn_in": w_ffn_in[i].astype(BF16),
            "w_ffn_out": w_ffn_out[i].astype(BF16),
            "w_ple": w_ple[i].astype(BF16),
            "w_ple_gate": w_ple_gate[i].astype(BF16),
            "g_pre_mix": g_pre_mix[i][None, :],
            "g_q": g_q[i][None, :],
            "g_k": g_k[i][None, :],
            "g_post_mix": g_post_mix[i][None, :],
            "g_pre_ffn": g_pre_ffn[i][None, :],
            "g_post_ffn": g_post_ffn[i][None, :],
            "g_ple": g_ple[i][None, :],
        }
        outs = []
        for h, p in ((h_p, p_prompt[i]), (h_s, p_sample[i])):
            seq = h.shape[1]
            w["cos"], w["sin_lo"], w["sin_hi"] = _rope_tables(seq)
            w["na_tables"] = _na_bias_tables(rpb[i], seq // GRID_W)
            outs.append(_layer(h, p, w, seq))
        h_p, h_s = outs
    return (h_p, h_s)
```

```python
import functools
import math

import numpy as np
import jax
import jax.numpy as jnp
from jax import lax
from jax.experimental import pallas as pl
from jax.experimental.pallas import tpu as pltpu

F32 = jnp.float32
BF16 = jnp.bfloat16

HEAD_DIM = 128
NA_HEADS = 8
GQA_Q_HEADS = 8
GQA_KV_HEADS = 2
GQA_GROUP = GQA_Q_HEADS // GQA_KV_HEADS
GRID_W = 64
NA_ROWS = 8
NA_COLS = 16
ROPE_THETA = 10000.0
RMS_EPS = 1e-6
NEG_INF = -1e30
LOG2E = math.log2(math.e)
Q_SCALE = LOG2E / math.sqrt(HEAD_DIM)

NA_WIDTH = NA_HEADS * HEAD_DIM
GQA_Q_WIDTH = GQA_Q_HEADS * HEAD_DIM
GQA_KV_WIDTH = GQA_KV_HEADS * HEAD_DIM
N_ATTN_A = 3 * NA_WIDTH
N_ATTN_B = GQA_Q_WIDTH + 2 * GQA_KV_WIDTH

NA_QROWS = 4
NA_WIN_ROWS = 12
NA_TQ = NA_QROWS * GRID_W
NA_TK = NA_WIN_ROWS * GRID_W
NA_BLOCKS_PER_STEP = 4

VMEM_LIMIT = 56 * 1024 * 1024


def _rms(x, g):
    ms = jnp.mean(x * x, axis=-1, keepdims=True)
    return x * lax.rsqrt(ms + RMS_EPS) * g


def _sigmoid(x):
    return 0.5 * jnp.tanh(0.5 * x) + 0.5


def _resident(shape):
    return pl.BlockSpec(shape, lambda *_: (0,) * len(shape), pipeline_mode=pl.Buffered(1))


def _rope(a, cos, sin_lo, sin_hi):
    return a * cos + pltpu.roll(a, 96, 1) * sin_lo + pltpu.roll(a, 32, 1) * sin_hi


CHAIN_ROWS = 256


def _row_chunks(n_rows):
    return [slice(r, r + CHAIN_ROWS) for r in range(0, n_rows, CHAIN_ROWS)]


def _qkv_b_kernel(x_ref, g_ref, w_ref, cos_ref, slo_ref, shi_ref, gq_ref, gk_ref,
                  xn_ref, o_ref):
    for rows in _row_chunks(x_ref.shape[0]):
        xn = _rms(x_ref[rows, :], g_ref[...]).astype(BF16)
        xn_ref[rows, :] = xn
        acc = jnp.dot(xn, w_ref[...], preferred_element_type=F32)
        cos, slo, shi = cos_ref[rows, :], slo_ref[rows, :], shi_ref[rows, :]
        for h in range(GQA_Q_HEADS + GQA_KV_HEADS):
            sl = slice(h * HEAD_DIM, (h + 1) * HEAD_DIM)
            if h < GQA_Q_HEADS:
                a = _rope(_rms(acc[:, sl], gq_ref[...]), cos, slo, shi) * Q_SCALE
            else:
                a = _rope(_rms(acc[:, sl], gk_ref[...]), cos, slo, shi)
            o_ref[rows, sl] = a.astype(BF16)
        v0 = GQA_Q_WIDTH + GQA_KV_WIDTH
        o_ref[rows, v0:] = acc[:, v0:].astype(BF16)


def _qkv_b(x2d, g_pre, w_in, cos, sin_lo, sin_hi, g_q, g_k, seq):
    t, d = x2d.shape
    tm = 1024
    n_pos_blocks = seq // tm
    pos_spec = pl.BlockSpec((tm, HEAD_DIM), lambda i: (i % n_pos_blocks, 0))
    return pl.pallas_call(
        _qkv_b_kernel,
        out_shape=(jax.ShapeDtypeStruct((t, d), BF16), jax.ShapeDtypeStruct((t, N_ATTN_B), BF16)),
        grid=(t // tm,),
        in_specs=[
            pl.BlockSpec((tm, d), lambda i: (i, 0)),
            _resident((1, d)),
            pl.BlockSpec((d, N_ATTN_B), lambda i: (0, N_ATTN_A // N_ATTN_B),
                         pipeline_mode=pl.Buffered(1)),
            pos_spec, pos_spec, pos_spec,
            _resident((1, HEAD_DIM)),
            _resident((1, HEAD_DIM)),
        ],
        out_specs=(pl.BlockSpec((tm, d), lambda i: (i, 0)),
                   pl.BlockSpec((tm, N_ATTN_B), lambda i: (i, 0))),
        compiler_params=pltpu.CompilerParams(
            dimension_semantics=("parallel",),
            vmem_limit_bytes=VMEM_LIMIT),
        name="qkv_b",
    )(x2d, g_pre, w_in, cos, sin_lo, sin_hi, g_q, g_k)


def _colscale_kernel(xn_ref, w_ref, cs_ref, o_ref):
    acc = jnp.dot(xn_ref[...], w_ref[...], preferred_element_type=F32)
    o_ref[...] = (acc * cs_ref[...]).astype(BF16)


def _xn_proj(body, name, xn, w, n, *extra):
    t, d = xn.shape
    tm, tn = 2048, 1024
    extra_specs = [pl.BlockSpec((1, tn), lambda i, j: (0, j)) for _ in extra]
    return pl.pallas_call(
        body,
        out_shape=jax.ShapeDtypeStruct((t, n), BF16),
        grid=(t // tm, n // tn),
        in_specs=[
            pl.BlockSpec((tm, d), lambda i, j: (i, 0)),
            pl.BlockSpec((d, tn), lambda i, j: (0, j)),
        ] + extra_specs,
        out_specs=pl.BlockSpec((tm, tn), lambda i, j: (i, j)),
        compiler_params=pltpu.CompilerParams(
            dimension_semantics=("parallel", "arbitrary"),
            vmem_limit_bytes=VMEM_LIMIT),
        name=name,
    )(xn, w, *extra)


def _na_window_start(blk, rows):
    return jnp.clip(blk * NA_QROWS - NA_ROWS // 2, 0, rows - NA_WIN_ROWS)


def _na_kernel(q_ref, k_ref, v_ref, tab_ref, o_ref, *, rows):
    n_blk = rows // NA_QROWS
    ones = jnp.ones((NA_TK, HEAD_DIM), BF16)
    for i in range(NA_BLOCKS_PER_STEP):
        blk = pl.program_id(1) * NA_BLOCKS_PER_STEP + i
        qrows = slice(i * NA_TQ, (i + 1) * NA_TQ)
        k0 = pl.multiple_of(_na_window_start(blk, rows) * GRID_W, GRID_W)
        variant = jnp.where(blk == 0, 0, jnp.where(blk == n_blk - 1, 2, 1))
        for h in range(NA_HEADS):
            sl = slice(h * HEAD_DIM, (h + 1) * HEAD_DIM)
            q = q_ref[qrows, sl]
            k = k_ref[pl.ds(k0, NA_TK), sl]
            v = v_ref[pl.ds(k0, NA_TK), sl]
            s = lax.dot_general(q, k, (((1,), (1,)), ((), ())), preferred_element_type=F32)
            s = s + tab_ref[variant, h]
            m = jnp.max(s, axis=-1, keepdims=True)
            p = jnp.exp2(s - m).astype(BF16)
            ol = jnp.dot(p, jnp.concatenate([v, ones], axis=1), preferred_element_type=F32)
            o_ref[qrows, sl] = (ol[:, :HEAD_DIM] * (1.0 / ol[:, HEAD_DIM:])).astype(BF16)


def _na_bias_tables(rpb, rows):
    n_blk = rows // NA_QROWS
    n_dr = 2 * NA_ROWS - 1
    qc = np.arange(GRID_W)[:, None]
    kc = np.arange(GRID_W)[None, :]
    dc = np.clip(kc - qc, -(NA_COLS - 1), NA_COLS - 1) + NA_COLS - 1
    onehot = (dc.reshape(-1)[None, :] == np.arange(2 * NA_COLS - 1)[:, None]).astype(np.float32)
    cexp = jnp.einsum("hrd,dx->hrx", rpb.astype(F32) * LOG2E, jnp.asarray(onehot),
                      precision=lax.Precision.HIGHEST)
    cexp = cexp.reshape(NA_HEADS, n_dr, GRID_W, GRID_W)
    pad = NA_WIN_ROWS
    cexp = jnp.pad(cexp, ((0, 0), (pad, pad), (0, 0), (0, 0)))
    cs = np.clip(qc - NA_COLS // 2, 0, GRID_W - NA_COLS)
    col_ok = (kc >= cs) & (kc < cs + NA_COLS)
    tabs = []
    for blk in (0, 1, n_blk - 1):
        r0 = blk * NA_QROWS
        k0 = int(np.clip(r0 - NA_ROWS // 2, 0, rows - NA_WIN_ROWS))
        per_row = []
        for qr in range(r0, r0 + NA_QROWS):
            start = k0 - qr + NA_ROWS - 1 + pad
            per_row.append(cexp[:, start:start + NA_WIN_ROWS])
        b = jnp.stack(per_row, axis=1)
        qr_ = (r0 + np.arange(NA_QROWS))[:, None]
        kr_ = (k0 + np.arange(NA_WIN_ROWS))[None, :]
        rs = np.clip(qr_ - NA_ROWS // 2, 0, rows - NA_ROWS)
        row_ok = (kr_ >= rs) & (kr_ < rs + NA_ROWS)
        ok = row_ok[:, :, None, None] & col_ok[None, None, :, :]
        b = jnp.where(jnp.asarray(ok)[None], b, NEG_INF)
        b = b.transpose(0, 1, 3, 2, 4).reshape(NA_HEADS, NA_TQ, NA_TK)
        tabs.append(b)
    return jnp.stack(tabs)


def _na_attention(pa, tables, batch, seq):
    rows = seq // GRID_W
    n_steps = rows // (NA_QROWS * NA_BLOCKS_PER_STEP)
    tq = NA_TQ * NA_BLOCKS_PER_STEP
    t = batch * seq

    return pl.pallas_call(
        functools.partial(_na_kernel, rows=rows),
        out_shape=jax.ShapeDtypeStruct((t, NA_WIDTH), BF16),
        grid=(batch, n_steps),
        in_specs=[
            pl.BlockSpec((tq, NA_WIDTH), lambda b, i: (b * n_steps + i, 0)),
            pl.BlockSpec((seq, NA_WIDTH), lambda b, i: (b, 1)),
            pl.BlockSpec((seq, NA_WIDTH), lambda b, i: (b, 2)),
            _resident(tables.shape),
        ],
        out_specs=pl.BlockSpec((tq, NA_WIDTH), lambda b, i: (b * n_steps + i, 0)),
        compiler_params=pltpu.CompilerParams(
            dimension_semantics=("parallel", "arbitrary"),
            vmem_limit_bytes=VMEM_LIMIT),
        name="na_attn",
    )(pa, pa, pa, tables)


GQA_TQ = 1024
GQA_CHAIN_ROWS = 128


def _gqa_kernel(q_ref, k_ref, v_ref, o_ref, vaug_ref):
    @pl.when(pl.program_id(2) == 0)
    def _():
        vaug_ref[:, :HEAD_DIM] = v_ref[...]
        vaug_ref[:, HEAD_DIM:] = jnp.ones((v_ref.shape[0], HEAD_DIM), BF16)

    k = k_ref[...]
    for r in range(GQA_TQ // GQA_CHAIN_ROWS):
        rows = slice(r * GQA_CHAIN_ROWS, (r + 1) * GQA_CHAIN_ROWS)
        for h in range(GQA_GROUP):
            sl = slice(h * HEAD_DIM, (h + 1) * HEAD_DIM)
            s = lax.dot_general(q_ref[rows, sl], k, (((1,), (1,)), ((), ())),
                                preferred_element_type=F32)
            m = jnp.max(s, axis=-1, keepdims=True)
            p = jnp.exp2(s - m).astype(BF16)
            ol = jnp.dot(p, vaug_ref[...], preferred_element_type=F32)
            o_ref[rows, sl] = (ol[:, :HEAD_DIM] * (1.0 / ol[:, HEAD_DIM:])).astype(BF16)


def _gqa_attention(pb, batch, seq):
    t = batch * seq
    n_q = seq // GQA_TQ
    gw = GQA_GROUP * HEAD_DIM
    return pl.pallas_call(
        _gqa_kernel,
        out_shape=jax.ShapeDtypeStruct((t, GQA_Q_WIDTH), BF16),
        grid=(batch, GQA_KV_HEADS, n_q),
        in_specs=[
            pl.BlockSpec((GQA_TQ, gw), lambda b, g, i: (b * n_q + i, g)),
            pl.BlockSpec((seq, HEAD_DIM), lambda b, g, i: (b, GQA_Q_WIDTH // HEAD_DIM + g)),
            pl.BlockSpec((seq, HEAD_DIM),
                         lambda b, g, i: (b, (GQA_Q_WIDTH + GQA_KV_WIDTH) // HEAD_DIM + g)),
        ],
        out_specs=pl.BlockSpec((GQA_TQ, gw), lambda b, g, i: (b * n_q + i, g)),
        scratch_shapes=[pltpu.VMEM((seq, 2 * HEAD_DIM), BF16)],
        compiler_params=pltpu.CompilerParams(
            dimension_semantics=("parallel", "parallel", "arbitrary"),
            vmem_limit_bytes=VMEM_LIMIT),
        name="gqa_attn",
    )(pb, pb, pb)


def _mix1_kernel(xn_ref, a_ref, b_ref, wga_ref, wgb_ref, wa_ref, wb_ref, o_ref):
    xn = xn_ref[...]
    ga = _sigmoid(jnp.dot(xn, wga_ref[...], preferred_element_type=F32))
    ya = jnp.dot(a_ref[...], wa_ref[...], preferred_element_type=F32)
    gb = _sigmoid(jnp.dot(xn, wgb_ref[...], preferred_element_type=F32))
    yb = jnp.dot(b_ref[...], wb_ref[...], preferred_element_type=F32)
    o_ref[...] = (ga * ya + gb * yb).astype(BF16)


def _mix1(xn, na_out, gqa_out, w_in, wa, wb):
    t, d = xn.shape
    tm, tn = 1024, 512
    n_j = d // tn
    j0 = (N_ATTN_A + N_ATTN_B) // tn
    return pl.pallas_call(
        _mix1_kernel,
        out_shape=jax.ShapeDtypeStruct((t, d), BF16),
        grid=(t // tm, n_j),
        in_specs=[
            pl.BlockSpec((tm, d), lambda i, j: (i, 0)),
            pl.BlockSpec((tm, NA_WIDTH), lambda i, j: (i, 0)),
            pl.BlockSpec((tm, GQA_Q_WIDTH), lambda i, j: (i, 0)),
            pl.BlockSpec((d, tn), lambda i, j: (0, j0 + j)),
            pl.BlockSpec((d, tn), lambda i, j: (0, j0 + n_j + j)),
            pl.BlockSpec((NA_WIDTH, tn), lambda i, j: (0, j)),
            pl.BlockSpec((GQA_Q_WIDTH, tn), lambda i, j: (0, j)),
        ],
        out_specs=pl.BlockSpec((tm, tn), lambda i, j: (i, j)),
        compiler_params=pltpu.CompilerParams(
            dimension_semantics=("parallel", "arbitrary"),
            vmem_limit_bytes=VMEM_LIMIT),
        name="mix1",
    )(xn, na_out, gqa_out, w_in, w_in, wa, wb)


def _mix2_kernel(m_ref, x_ref, wo_ref, gpost_ref, gpre_ref, h_ref, xn_ref):
    for rows in _row_chunks(m_ref.shape[0]):
        m2 = jnp.dot(m_ref[rows, :], wo_ref[...], preferred_element_type=F32)
        h = x_ref[rows, :] + _rms(m2, gpost_ref[...])
        h_ref[rows, :] = h
        xn_ref[rows, :] = _rms(h, gpre_ref[...]).astype(BF16)


def _mix2(mixed, x2d, wo, g_post, g_pre_ffn):
    t, d = x2d.shape
    tm = 512
    row_spec = pl.BlockSpec((tm, d), lambda i: (i, 0))
    return pl.pallas_call(
        _mix2_kernel,
        out_shape=(jax.ShapeDtypeStruct((t, d), F32), jax.ShapeDtypeStruct((t, d), BF16)),
        grid=(t // tm,),
        in_specs=[row_spec, row_spec, _resident(wo.shape), _resident((1, d)), _resident((1, d))],
        out_specs=(row_spec, row_spec),
        compiler_params=pltpu.CompilerParams(
            dimension_semantics=("parallel",),
            vmem_limit_bytes=VMEM_LIMIT),
        name="mix2",
    )(mixed, x2d, wo, g_post, g_pre_ffn)


FFN_TF = 512


def _ffn_kernel(xn_ref, wg_ref, wu_ref, wo_ref, f_ref, acc_ref):
    j = pl.program_id(1)

    @pl.when((pl.program_id(0) == 0) & (j == 0))
    def _():
        acc_ref[...] = jnp.zeros_like(acc_ref)

    xn = xn_ref[...]
    g = jnp.dot(xn, wg_ref[...], preferred_element_type=F32)
    u = jnp.dot(xn, wu_ref[...], preferred_element_type=F32)
    act = (g * _sigmoid(g) * u).astype(BF16)
    prev = jnp.where(j == 0, 0.0, acc_ref[...])
    acc = prev + jnp.dot(act, wo_ref[...], preferred_element_type=F32)
    acc_ref[...] = acc
    f_ref[...] = acc.astype(BF16)


def _ffn(xn2, w_ffn_in, w_ffn_out):
    t, d = xn2.shape
    d_ff = w_ffn_out.shape[0]
    tm = 1024
    n_f = d_ff // FFN_TF
    return pl.pallas_call(
        _ffn_kernel,
        out_shape=jax.ShapeDtypeStruct((t, d), BF16),
        grid=(t // tm, n_f),
        in_specs=[
            pl.BlockSpec((tm, d), lambda i, j: (i, 0)),
            pl.BlockSpec((d, FFN_TF), lambda i, j: (0, j)),
            pl.BlockSpec((d, FFN_TF), lambda i, j: (0, n_f + j)),
            pl.BlockSpec((FFN_TF, d), lambda i, j: (j, 0)),
        ],
        out_specs=pl.BlockSpec((tm, d), lambda i, j: (i, 0)),
        scratch_shapes=[pltpu.VMEM((tm, d), F32)],
        compiler_params=pltpu.CompilerParams(
            dimension_semantics=("arbitrary", "arbitrary"),
            vmem_limit_bytes=VMEM_LIMIT),
        name="ffn",
    )(xn2, w_ffn_in, w_ffn_in, w_ffn_out)


def _ple_kernel(h_ref, f_ref, p_ref, wple_ref, wgate_ref, gffn_ref, gple_ref, y_ref):
    for rows in _row_chunks(h_ref.shape[0]):
        h = h_ref[rows, :] + _rms(f_ref[rows, :].astype(F32), gffn_ref[...])
        e = jnp.dot(p_ref[rows, :].astype(BF16), wple_ref[...], preferred_element_type=F32)
        gate = _sigmoid(jnp.dot(h.astype(BF16), wgate_ref[...], preferred_element_type=F32))
        y_ref[rows, :] = h + _rms(gate * e, gple_ref[...])


def _ple(h1, f, p2d, w_ple, w_gate, g_ffn, g_ple):
    t, d = h1.shape
    pd = p2d.shape[1]
    tm = 512
    return pl.pallas_call(
        _ple_kernel,
        out_shape=jax.ShapeDtypeStruct((t, d), F32),
        grid=(t // tm,),
        in_specs=[
            pl.BlockSpec((tm, d), lambda i: (i, 0)),
            pl.BlockSpec((tm, d), lambda i: (i, 0)),
            pl.BlockSpec((tm, pd), lambda i: (i, 0)),
            _resident(w_ple.shape),
            _resident(w_gate.shape),
            _resident((1, d)),
            _resident((1, d)),
        ],
        out_specs=pl.BlockSpec((tm, d), lambda i: (i, 0)),
        compiler_params=pltpu.CompilerParams(
            dimension_semantics=("parallel",),
            vmem_limit_bytes=VMEM_LIMIT),
        name="ple",
    )(h1, f, p2d, w_ple, w_gate, g_ffn, g_ple)


def _rope_tables(seq):
    half = HEAD_DIM // 2
    t = np.arange(seq)
    inv_freq = ROPE_THETA ** (-np.arange(0, half, 2, dtype=np.float64) / half)

    def tables(pos):
        ang = pos.astype(np.float64)[:, None] * inv_freq[None, :]
        return (np.concatenate([np.cos(ang), np.cos(ang)], axis=-1),
                np.concatenate([np.sin(ang), np.sin(ang)], axis=-1))

    cos_r, sin_r = tables(t // GRID_W)
    cos_c, sin_c = tables(t % GRID_W)
    cos = np.concatenate([cos_r, cos_c], axis=-1)
    sin = np.concatenate([sin_r, sin_c], axis=-1)
    first = (np.arange(HEAD_DIM) % half) < half // 2
    sin_lo = np.where(first[None, :], -sin, 0.0)
    sin_hi = np.where(first[None, :], 0.0, sin)
    return tuple(jnp.asarray(a, F32) for a in (cos, sin_lo, sin_hi))


def _layer(x, p, w, seq):
    batch = x.shape[0]
    d = x.shape[-1]
    x2d = x.reshape(batch * seq, d)
    p2d = p.reshape(batch * seq, p.shape[-1])
    xn, pb = _qkv_b(x2d, w["g_pre_mix"], w["w_in"], w["cos"], w["sin_lo"], w["sin_hi"],
                    w["g_q"], w["g_k"], seq)
    pa = _xn_proj(_colscale_kernel, "qkv_a", xn, w["w_in"], N_ATTN_A, w["qkv_a_scale"])
    na_out = _na_attention(pa, w["na_tables"], batch, seq)
    gqa_out = _gqa_attention(pb, batch, seq)
    mixed = _mix1(xn, na_out, gqa_out, w["w_in"], w["w_branch_a"], w["w_branch_b"])
    h1, xn2 = _mix2(mixed, x2d, w["w_out"], w["g_post_mix"], w["g_pre_ffn"])
    f = _ffn(xn2, w["w_ffn_in"], w["w_ffn_out"])
    y = _ple(h1, f, p2d, w["w_ple"], w["w_ple_gate"], w["g_post_ffn"], w["g_ple"])
    return y.reshape(x.shape)


def kernel(x_prompt, x_sample, p_prompt, p_sample, g_pre_mix, w_in, rpb, g_q, g_k, w_branch_a, w_branch_b, w_out, g_post_mix, g_pre_ffn, w_ffn_in, w_ffn_out, g_post_ffn, w_ple, w_ple_gate, g_ple):
    depth = w_in.shape[0]
    h_p, h_s = x_prompt, x_sample
    qkv_a_scale = jnp.concatenate(
        [jnp.full((1, NA_WIDTH), Q_SCALE, F32), jnp.ones((1, 2 * NA_WIDTH), F32)], axis=1)
    for i in range(depth):
        w = {
            "w_in": w_in[i].astype(BF16),
            "qkv_a_scale": qkv_a_scale,
            "w_branch_a": w_branch_a[i].astype(BF16),
            "w_branch_b": w_branch_b[i].astype(BF16),
            "w_out": w_out[i].astype(BF16),
            "w_ffn_in": w_ffn_in[i].astype(BF16),
            "w_ffn_out": w_ffn_out[i].astype(BF16),
            "w_ple": w_ple[i].astype(BF16),
            "w_ple_gate": w_ple_gate[i].astype(BF16),
            "g_pre_mix": g_pre_mix[i][None, :],
            "g_q": g_q[i][None, :],
            "g_k": g_k[i][None, :],
            "g_post_mix": g_post_mix[i][None, :],
            "g_pre_ffn": g_pre_ffn[i][None, :],
            "g_post_ffn": g_post_ffn[i][None, :],
            "g_ple": g_ple[i][None, :],
        }
        outs = []
        for h, p in ((h_p, p_prompt[i]), (h_s, p_sample[i])):
            seq = h.shape[1]
            w["cos"], w["sin_lo"], w["sin_hi"] = _rope_tables(seq)
            w["na_tables"] = _na_bias_tables(rpb[i], seq // GRID_W)
            outs.append(_layer(h, p, w, seq))
        h_p, h_s = outs
    return (h_p, h_s)
```

```python
import functools
import math

import numpy as np
import jax
import jax.numpy as jnp
from jax import lax
from jax.experimental import pallas as pl
from jax.experimental.pallas import tpu as pltpu

F32 = jnp.float32
BF16 = jnp.bfloat16

HEAD_DIM = 128
NA_HEADS = 8
GQA_Q_HEADS = 8
GQA_KV_HEADS = 2
GQA_GROUP = GQA_Q_HEADS // GQA_KV_HEADS
GRID_W = 64
NA_ROWS = 8
NA_COLS = 16
ROPE_THETA = 10000.0
RMS_EPS = 1e-6
NEG_INF = -1e30
LOG2E = math.log2(math.e)
Q_SCALE = LOG2E / math.sqrt(HEAD_DIM)

NA_WIDTH = NA_HEADS * HEAD_DIM
GQA_Q_WIDTH = GQA_Q_HEADS * HEAD_DIM
GQA_KV_WIDTH = GQA_KV_HEADS * HEAD_DIM
N_ATTN_A = 3 * NA_WIDTH
N_ATTN_B = GQA_Q_WIDTH + 2 * GQA_KV_WIDTH

NA_QROWS = 4
NA_WIN_ROWS = 12
NA_TQ = NA_QROWS * GRID_W
NA_TK = NA_WIN_ROWS * GRID_W
NA_BLOCKS_PER_STEP = 4

V7X_VMEM_BYTES = 64 * 1024 * 1024
V7X_MXU_DIM = 256
VMEM_LIMIT = V7X_VMEM_BYTES // 8 * 7

CHAIN_ROWS = V7X_MXU_DIM
QKV_B_TM = 1024
XN_PROJ_TM, XN_PROJ_TN = 2048, 1024
GQA_TQ = 1024
GQA_CHAIN_ROWS = V7X_MXU_DIM // 2
MIX1_TM, MIX1_TN = 1024, 512
MIX2_TM = 512
FFN_TM, FFN_TF = 1024, 512
PLE_TM = 512


def _exact_div(a, b):
    assert a % b == 0, (a, b)
    return a // b


def _rms(x, g):
    ms = jnp.mean(x * x, axis=-1, keepdims=True)
    return x * lax.rsqrt(ms + RMS_EPS) * g


def _sigmoid(x):
    return 0.5 * jnp.tanh(0.5 * x) + 0.5


def _resident(shape):
    return pl.BlockSpec(shape, lambda *_: (0,) * len(shape), pipeline_mode=pl.Buffered(1))


def _rope(a, cos, sin_lo, sin_hi):
    return a * cos + pltpu.roll(a, 96, 1) * sin_lo + pltpu.roll(a, 32, 1) * sin_hi


def _row_chunks(n_rows):
    return [slice(r, r + CHAIN_ROWS) for r in range(0, n_rows, CHAIN_ROWS)]


def _tail_chunks(n_rows):
    last = CHAIN_ROWS // 2
    return [slice(0, n_rows - last), slice(n_rows - last, n_rows)]


def _qkv_b_kernel(x_ref, g_ref, w_ref, cos_ref, slo_ref, shi_ref, gq_ref, gk_ref,
                  xn_ref, o_ref):
    for rows in _row_chunks(x_ref.shape[0]):
        xn = _rms(x_ref[rows, :], g_ref[...]).astype(BF16)
        xn_ref[rows, :] = xn
        acc = jnp.dot(xn, w_ref[...], preferred_element_type=F32)
        cos, slo, shi = cos_ref[rows, :], slo_ref[rows, :], shi_ref[rows, :]
        for h in range(GQA_Q_HEADS + GQA_KV_HEADS):
            sl = slice(h * HEAD_DIM, (h + 1) * HEAD_DIM)
            if h < GQA_Q_HEADS:
                a = _rope(_rms(acc[:, sl], gq_ref[...]), cos, slo, shi) * Q_SCALE
            else:
                a = _rope(_rms(acc[:, sl], gk_ref[...]), cos, slo, shi)
            o_ref[rows, sl] = a.astype(BF16)
        v0 = GQA_Q_WIDTH + GQA_KV_WIDTH
        o_ref[rows, v0:] = acc[:, v0:].astype(BF16)


def _qkv_b(x2d, g_pre, w_in, cos, sin_lo, sin_hi, g_q, g_k, seq):
    t, d = x2d.shape
    tm = QKV_B_TM
    n_pos_blocks = _exact_div(seq, tm)
    pos_spec = pl.BlockSpec((tm, HEAD_DIM), lambda i: (i % n_pos_blocks, 0))
    return pl.pallas_call(
        _qkv_b_kernel,
        out_shape=(jax.ShapeDtypeStruct((t, d), BF16), jax.ShapeDtypeStruct((t, N_ATTN_B), BF16)),
        grid=(_exact_div(t, tm),),
        in_specs=[
            pl.BlockSpec((tm, d), lambda i: (i, 0)),
            _resident((1, d)),
            pl.BlockSpec((d, N_ATTN_B), lambda i: (0, _exact_div(N_ATTN_A, N_ATTN_B)),
                         pipeline_mode=pl.Buffered(1)),
            pos_spec, pos_spec, pos_spec,
            _resident((1, HEAD_DIM)),
            _resident((1, HEAD_DIM)),
        ],
        out_specs=(pl.BlockSpec((tm, d), lambda i: (i, 0)),
                   pl.BlockSpec((tm, N_ATTN_B), lambda i: (i, 0))),
        compiler_params=pltpu.CompilerParams(
            dimension_semantics=("parallel",),
            vmem_limit_bytes=VMEM_LIMIT),
        name="qkv_b",
    )(x2d, g_pre, w_in, cos, sin_lo, sin_hi, g_q, g_k)


def _colscale_kernel(xn_ref, w_ref, cs_ref, o_ref):
    acc = jnp.dot(xn_ref[...], w_ref[...], preferred_element_type=F32)
    o_ref[...] = (acc * cs_ref[...]).astype(BF16)


def _xn_proj(body, name, xn, w, n, *extra):
    t, d = xn.shape
    tm, tn = XN_PROJ_TM, XN_PROJ_TN
    extra_specs = [pl.BlockSpec((1, tn), lambda i, j: (0, j)) for _ in extra]
    return pl.pallas_call(
        body,
        out_shape=jax.ShapeDtypeStruct((t, n), BF16),
        grid=(_exact_div(t, tm), _exact_div(n, tn)),
        in_specs=[
            pl.BlockSpec((tm, d), lambda i, j: (i, 0)),
            pl.BlockSpec((d, tn), lambda i, j: (0, j)),
        ] + extra_specs,
        out_specs=pl.BlockSpec((tm, tn), lambda i, j: (i, j)),
        compiler_params=pltpu.CompilerParams(
            dimension_semantics=("parallel", "arbitrary"),
            vmem_limit_bytes=VMEM_LIMIT),
        name=name,
    )(xn, w, *extra)


def _na_window_start(blk, rows):
    return jnp.clip(blk * NA_QROWS - NA_ROWS // 2, 0, rows - NA_WIN_ROWS)


def _na_kernel(q_ref, k_ref, v_ref, tab_ref, o_ref, *, rows):
    n_blk = rows // NA_QROWS
    ones = jnp.ones((NA_TK, HEAD_DIM), BF16)
    for i in range(NA_BLOCKS_PER_STEP):
        blk = pl.program_id(1) * NA_BLOCKS_PER_STEP + i
        qrows = slice(i * NA_TQ, (i + 1) * NA_TQ)
        k0 = pl.multiple_of(_na_window_start(blk, rows) * GRID_W, GRID_W)
        variant = jnp.where(blk == 0, 0, jnp.where(blk == n_blk - 1, 2, 1))
        for h in range(NA_HEADS):
            sl = slice(h * HEAD_DIM, (h + 1) * HEAD_DIM)
            q = q_ref[qrows, sl]
            k = k_ref[pl.ds(k0, NA_TK), sl]
            v = v_ref[pl.ds(k0, NA_TK), sl]
            s = lax.dot_general(q, k, (((1,), (1,)), ((), ())), preferred_element_type=F32)
            s = s + tab_ref[variant, h]
            m = jnp.max(s, axis=-1, keepdims=True)
            p = jnp.exp2(s - m).astype(BF16)
            ol = jnp.dot(p, jnp.concatenate([v, ones], axis=1), preferred_element_type=F32)
            o_ref[qrows, sl] = (ol[:, :HEAD_DIM] * (1.0 / ol[:, HEAD_DIM:])).astype(BF16)


def _na_bias_tables(rpb, rows):
    n_blk = rows // NA_QROWS
    qc = np.arange(GRID_W)[:, None]
    kc = np.arange(GRID_W)[None, :]
    dc = np.clip(kc - qc, -(NA_COLS - 1), NA_COLS - 1) + NA_COLS - 1
    onehot = (dc[None] == np.arange(2 * NA_COLS - 1)[:, None, None]).astype(np.float32)
    cexp = jnp.einsum("hrd,dqk->hqrk", rpb.astype(F32) * LOG2E, jnp.asarray(onehot),
                      precision=lax.Precision.HIGHEST)
    pad = NA_WIN_ROWS
    cexp = jnp.pad(cexp, ((0, 0), (0, 0), (pad, pad), (0, 0)))
    cs = np.clip(qc - NA_COLS // 2, 0, GRID_W - NA_COLS)
    col_ok = (kc >= cs) & (kc < cs + NA_COLS)
    tabs = []
    for blk in (0, 1, n_blk - 1):
        r0 = blk * NA_QROWS
        k0 = int(np.clip(r0 - NA_ROWS // 2, 0, rows - NA_WIN_ROWS))
        per_row = []
        for qr in range(r0, r0 + NA_QROWS):
            start = k0 - qr + NA_ROWS - 1 + pad
            per_row.append(cexp[:, :, start:start + NA_WIN_ROWS])
        b = jnp.stack(per_row, axis=1)
        qr_ = (r0 + np.arange(NA_QROWS))[:, None]
        kr_ = (k0 + np.arange(NA_WIN_ROWS))[None, :]
        rs = np.clip(qr_ - NA_ROWS // 2, 0, rows - NA_ROWS)
        row_ok = (kr_ >= rs) & (kr_ < rs + NA_ROWS)
        ok = row_ok[:, None, :, None] & col_ok[None, :, None, :]
        b = jnp.where(jnp.asarray(ok)[None], b, NEG_INF)
        tabs.append(b.reshape(NA_HEADS, NA_TQ, NA_TK))
    return jnp.stack(tabs)


def _na_attention(pa, tables, batch, seq):
    rows = seq // GRID_W
    n_steps = _exact_div(rows, NA_QROWS * NA_BLOCKS_PER_STEP)
    tq = NA_TQ * NA_BLOCKS_PER_STEP
    t = batch * seq

    return pl.pallas_call(
        functools.partial(_na_kernel, rows=rows),
        out_shape=jax.ShapeDtypeStruct((t, NA_WIDTH), BF16),
        grid=(batch, n_steps),
        in_specs=[
            pl.BlockSpec((tq, NA_WIDTH), lambda b, i: (b * n_steps + i, 0)),
            pl.BlockSpec((seq, NA_WIDTH), lambda b, i: (b, 1)),
            pl.BlockSpec((seq, NA_WIDTH), lambda b, i: (b, 2)),
            _resident(tables.shape),
        ],
        out_specs=pl.BlockSpec((tq, NA_WIDTH), lambda b, i: (b * n_steps + i, 0)),
        compiler_params=pltpu.CompilerParams(
            dimension_semantics=("parallel", "arbitrary"),
            vmem_limit_bytes=VMEM_LIMIT),
        name="na_attn",
    )(pa, pa, pa, tables)


def _gqa_kernel(q_ref, k_ref, v_ref, o_ref, vaug_ref):
    @pl.when(pl.program_id(2) == 0)
    def _():
        vaug_ref[:, :HEAD_DIM] = v_ref[...]
        vaug_ref[:, HEAD_DIM:] = jnp.ones((v_ref.shape[0], HEAD_DIM), BF16)

    k = k_ref[...]
    for r in range(GQA_TQ // GQA_CHAIN_ROWS):
        rows = slice(r * GQA_CHAIN_ROWS, (r + 1) * GQA_CHAIN_ROWS)
        for h in range(GQA_GROUP):
            sl = slice(h * HEAD_DIM, (h + 1) * HEAD_DIM)
            s = lax.dot_general(q_ref[rows, sl], k, (((1,), (1,)), ((), ())),
                                preferred_element_type=F32)
            m = jnp.max(s, axis=-1, keepdims=True)
            p = jnp.exp2(s - m).astype(BF16)
            ol = jnp.dot(p, vaug_ref[...], preferred_element_type=F32)
            o_ref[rows, sl] = (ol[:, :HEAD_DIM] * (1.0 / ol[:, HEAD_DIM:])).astype(BF16)


def _gqa_attention(pb, batch, seq):
    t = batch * seq
    n_q = _exact_div(seq, GQA_TQ)
    gw = GQA_GROUP * HEAD_DIM
    return pl.pallas_call(
        _gqa_kernel,
        out_shape=jax.ShapeDtypeStruct((t, GQA_Q_WIDTH), BF16),
        grid=(batch, GQA_KV_HEADS, n_q),
        in_specs=[
            pl.BlockSpec((GQA_TQ, gw), lambda b, g, i: (b * n_q + i, g)),
            pl.BlockSpec((seq, HEAD_DIM), lambda b, g, i: (b, GQA_Q_WIDTH // HEAD_DIM + g)),
            pl.BlockSpec((seq, HEAD_DIM),
                         lambda b, g, i: (b, (GQA_Q_WIDTH + GQA_KV_WIDTH) // HEAD_DIM + g)),
        ],
        out_specs=pl.BlockSpec((GQA_TQ, gw), lambda b, g, i: (b * n_q + i, g)),
        scratch_shapes=[pltpu.VMEM((seq, 2 * HEAD_DIM), BF16)],
        compiler_params=pltpu.CompilerParams(
            dimension_semantics=("parallel", "parallel", "arbitrary"),
            vmem_limit_bytes=VMEM_LIMIT),
        name="gqa_attn",
    )(pb, pb, pb)


def _mix1_kernel(xn_ref, a_ref, b_ref, wga_ref, wgb_ref, wa_ref, wb_ref, o_ref):
    xn = xn_ref[...]
    ga = _sigmoid(jnp.dot(xn, wga_ref[...], preferred_element_type=F32))
    ya = jnp.dot(a_ref[...], wa_ref[...], preferred_element_type=F32)
    gb = _sigmoid(jnp.dot(xn, wgb_ref[...], preferred_element_type=F32))
    yb = jnp.dot(b_ref[...], wb_ref[...], preferred_element_type=F32)
    o_ref[...] = (ga * ya + gb * yb).astype(BF16)


def _mix1(xn, na_out, gqa_out, w_in, wa, wb):
    t, d = xn.shape
    tm, tn = MIX1_TM, MIX1_TN
    n_j = _exact_div(d, tn)
    j0 = _exact_div(N_ATTN_A + N_ATTN_B, tn)
    return pl.pallas_call(
        _mix1_kernel,
        out_shape=jax.ShapeDtypeStruct((t, d), BF16),
        grid=(_exact_div(t, tm), n_j),
        in_specs=[
            pl.BlockSpec((tm, d), lambda i, j: (i, 0)),
            pl.BlockSpec((tm, NA_WIDTH), lambda i, j: (i, 0)),
            pl.BlockSpec((tm, GQA_Q_WIDTH), lambda i, j: (i, 0)),
            pl.BlockSpec((d, tn), lambda i, j: (0, j0 + j)),
            pl.BlockSpec((d, tn), lambda i, j: (0, j0 + n_j + j)),
            pl.BlockSpec((NA_WIDTH, tn), lambda i, j: (0, j)),
            pl.BlockSpec((GQA_Q_WIDTH, tn), lambda i, j: (0, j)),
        ],
        out_specs=pl.BlockSpec((tm, tn), lambda i, j: (i, j)),
        compiler_params=pltpu.CompilerParams(
            dimension_semantics=("parallel", "arbitrary"),
            vmem_limit_bytes=VMEM_LIMIT),
        name="mix1",
    )(xn, na_out, gqa_out, w_in, w_in, wa, wb)


def _mix2_kernel(m_ref, x_ref, wo_ref, gpost_ref, gpre_ref, h_ref, xn_ref):
    for rows in _tail_chunks(m_ref.shape[0]):
        m2 = jnp.dot(m_ref[rows, :], wo_ref[...], preferred_element_type=F32)
        h = x_ref[rows, :] + _rms(m2, gpost_ref[...])
        h_ref[rows, :] = h
        xn_ref[rows, :] = _rms(h, gpre_ref[...]).astype(BF16)


def _mix2(mixed, x2d, wo, g_post, g_pre_ffn):
    t, d = x2d.shape
    tm = MIX2_TM
    row_spec = pl.BlockSpec((tm, d), lambda i: (i, 0))
    return pl.pallas_call(
        _mix2_kernel,
        out_shape=(jax.ShapeDtypeStruct((t, d), F32), jax.ShapeDtypeStruct((t, d), BF16)),
        grid=(_exact_div(t, tm),),
        in_specs=[row_spec, row_spec, _resident(wo.shape), _resident((1, d)), _resident((1, d))],
        out_specs=(row_spec, row_spec),
        compiler_params=pltpu.CompilerParams(
            dimension_semantics=("parallel",),
            vmem_limit_bytes=VMEM_LIMIT),
        name="mix2",
    )(mixed, x2d, wo, g_post, g_pre_ffn)


def _ffn_kernel(xn_ref, wg_ref, wu_ref, wo_ref, f_ref, acc_ref):
    j = pl.program_id(1)

    @pl.when((pl.program_id(0) == 0) & (j == 0))
    def _():
        acc_ref[...] = jnp.zeros_like(acc_ref)

    xn = xn_ref[...]
    g = jnp.dot(xn, wg_ref[...], preferred_element_type=F32)
    u = jnp.dot(xn, wu_ref[...], preferred_element_type=F32)
    act = (g * _sigmoid(g) * u).astype(BF16)
    prev = jnp.where(j == 0, 0.0, acc_ref[...])
    acc = prev + jnp.dot(act, wo_ref[...], preferred_element_type=F32)
    acc_ref[...] = acc
    f_ref[...] = acc.astype(BF16)


def _ffn(xn2, w_ffn_in, w_ffn_out):
    t, d = xn2.shape
    d_ff = w_ffn_out.shape[0]
    tm = FFN_TM
    n_f = _exact_div(d_ff, FFN_TF)
    return pl.pallas_call(
        _ffn_kernel,
        out_shape=jax.ShapeDtypeStruct((t, d), BF16),
        grid=(_exact_div(t, tm), n_f),
        in_specs=[
            pl.BlockSpec((tm, d), lambda i, j: (i, 0)),
            pl.BlockSpec((d, FFN_TF), lambda i, j: (0, j)),
            pl.BlockSpec((d, FFN_TF), lambda i, j: (0, n_f + j)),
            pl.BlockSpec((FFN_TF, d), lambda i, j: (j, 0)),
        ],
        out_specs=pl.BlockSpec((tm, d), lambda i, j: (i, 0)),
        scratch_shapes=[pltpu.VMEM((tm, d), F32)],
        compiler_params=pltpu.CompilerParams(
            dimension_semantics=("arbitrary", "arbitrary"),
            vmem_limit_bytes=VMEM_LIMIT),
        name="ffn",
    )(xn2, w_ffn_in, w_ffn_in, w_ffn_out)


def _ple_kernel(h_ref, f_ref, p_ref, wple_ref, wgate_ref, gffn_ref, gple_ref, y_ref):
    for rows in _row_chunks(h_ref.shape[0]):
        h = h_ref[rows, :] + _rms(f_ref[rows, :].astype(F32), gffn_ref[...])
        e = jnp.dot(p_ref[rows, :].astype(BF16), wple_ref[...], preferred_element_type=F32)
        gate = _sigmoid(jnp.dot(h.astype(BF16), wgate_ref[...], preferred_element_type=F32))
        y_ref[rows, :] = h + _rms(gate * e, gple_ref[...])


def _ple(h1, f, p2d, w_ple, w_gate, g_ffn, g_ple):
    t, d = h1.shape
    pd = p2d.shape[1]
    tm = PLE_TM
    return pl.pallas_call(
        _ple_kernel,
        out_shape=jax.ShapeDtypeStruct((t, d), F32),
        grid=(_exact_div(t, tm),),
        in_specs=[
            pl.BlockSpec((tm, d), lambda i: (i, 0)),
            pl.BlockSpec((tm, d), lambda i: (i, 0)),
            pl.BlockSpec((tm, pd), lambda i: (i, 0)),
            _resident(w_ple.shape),
            _resident(w_gate.shape),
            _resident((1, d)),
            _resident((1, d)),
        ],
        out_specs=pl.BlockSpec((tm, d), lambda i: (i, 0)),
        compiler_params=pltpu.CompilerParams(
            dimension_semantics=("parallel",),
            vmem_limit_bytes=VMEM_LIMIT),
        name="ple",
    )(h1, f, p2d, w_ple, w_gate, g_ffn, g_ple)


def _rope_tables(seq):
    half = HEAD_DIM // 2
    t = np.arange(seq)
    inv_freq = ROPE_THETA ** (-np.arange(0, half, 2, dtype=np.float64) / half)

    def tables(pos):
        ang = pos.astype(np.float64)[:, None] * inv_freq[None, :]
        return (np.concatenate([np.cos(ang), np.cos(ang)], axis=-1),
                np.concatenate([np.sin(ang), np.sin(ang)], axis=-1))

    cos_r, sin_r = tables(t // GRID_W)
    cos_c, sin_c = tables(t % GRID_W)
    cos = np.concatenate([cos_r, cos_c], axis=-1)
    sin = np.concatenate([sin_r, sin_c], axis=-1)
    first = (np.arange(HEAD_DIM) % half) < half // 2
    sin_lo = np.where(first[None, :], -sin, 0.0)
    sin_hi = np.where(first[None, :], 0.0, sin)
    return tuple(jnp.asarray(a, F32) for a in (cos, sin_lo, sin_hi))


def _layer(x, p, w, seq):
    batch = x.shape[0]
    d = x.shape[-1]
    x2d = x.reshape(batch * seq, d)
    p2d = p.reshape(batch * seq, p.shape[-1])
    xn, pb = _qkv_b(x2d, w["g_pre_mix"], w["w_in"], w["cos"], w["sin_lo"], w["sin_hi"],
                    w["g_q"], w["g_k"], seq)
    pa = _xn_proj(_colscale_kernel, "qkv_a", xn, w["w_in"], N_ATTN_A, w["qkv_a_scale"])
    na_out = _na_attention(pa, w["na_tables"], batch, seq)
    gqa_out = _gqa_attention(pb, batch, seq)
    mixed = _mix1(xn, na_out, gqa_out, w["w_in"], w["w_branch_a"], w["w_branch_b"])
    h1, xn2 = _mix2(mixed, x2d, w["w_out"], w["g_post_mix"], w["g_pre_ffn"])
    f = _ffn(xn2, w["w_ffn_in"], w["w_ffn_out"])
    y = _ple(h1, f, p2d, w["w_ple"], w["w_ple_gate"], w["g_post_ffn"], w["g_ple"])
    return y.reshape(x.shape)


def kernel(x_prompt, x_sample, p_prompt, p_sample, g_pre_mix, w_in, rpb, g_q, g_k, w_branch_a, w_branch_b, w_out, g_post_mix, g_pre_ffn, w_ffn_in, w_ffn_out, g_post_ffn, w_ple, w_ple_gate, g_ple):
    depth = w_in.shape[0]
    h_p, h_s = x_prompt, x_sample
    qkv_a_scale = jnp.concatenate(
        [jnp.full((1, NA_WIDTH), Q_SCALE, F32), jnp.ones((1, 2 * NA_WIDTH), F32)], axis=1)
    for i in range(depth):
        w = {
            "w_in": w_in[i].astype(BF16),
            "qkv_a_scale": qkv_a_scale,
            "w_branch_a": w_branch_a[i].astype(BF16),
            "w_branch_b": w_branch_b[i].astype(BF16),
            "w_out": w_out[i].astype(BF16),
            "w_ffn_in": w_ffn_in[i].astype(BF16),
            "w_ffn_out": w_ffn_out[i].astype(BF16),
            "w_ple": w_ple[i].astype(BF16),
            "w_ple_gate": w_ple_gate[i].astype(BF16),
            "g_pre_mix": g_pre_mix[i][None, :],
            "g_q": g_q[i][None, :],
            "g_k": g_k[i][None, :],
            "g_post_mix": g_post_mix[i][None, :],
            "g_pre_ffn": g_pre_ffn[i][None, :],
            "g_post_ffn": g_post_ffn[i][None, :],
            "g_ple": g_ple[i][None, :],
        }
        outs = []
        for h, p in ((h_p, p_prompt[i]), (h_s, p_sample[i])):
            seq = h.shape[1]
            w["cos"], w["sin_lo"], w["sin_hi"] = _rope_tables(seq)
            w["na_tables"] = _na_bias_tables(rpb[i], seq // GRID_W)
            outs.append(_layer(h, p, w, seq))
        h_p, h_s = outs
    return (h_p, h_s)
```

```python
import functools
import math

import numpy as np
import jax
import jax.numpy as jnp
from jax import lax
from jax.experimental import pallas as pl
from jax.experimental.pallas import tpu as pltpu

F32 = jnp.float32
BF16 = jnp.bfloat16

HEAD_DIM = 128
NA_HEADS = 8
GQA_Q_HEADS = 8
GQA_KV_HEADS = 2
GQA_GROUP = GQA_Q_HEADS // GQA_KV_HEADS
GRID_W = 64
NA_ROWS = 8
NA_COLS = 16
ROPE_THETA = 10000.0
RMS_EPS = 1e-6
NEG_INF = -1e30
LOG2E = math.log2(math.e)
Q_SCALE = LOG2E / math.sqrt(HEAD_DIM)

NA_WIDTH = NA_HEADS * HEAD_DIM
GQA_Q_WIDTH = GQA_Q_HEADS * HEAD_DIM
GQA_KV_WIDTH = GQA_KV_HEADS * HEAD_DIM
N_ATTN_A = 3 * NA_WIDTH
N_ATTN_B = GQA_Q_WIDTH + 2 * GQA_KV_WIDTH

NA_QROWS = 4
NA_WIN_ROWS = 12
NA_TQ = NA_QROWS * GRID_W
NA_TK = NA_WIN_ROWS * GRID_W
NA_BLOCKS_PER_STEP = 4

V7X_VMEM_BYTES = 64 * 1024 * 1024
V7X_MXU_DIM = 256
VMEM_LIMIT = V7X_VMEM_BYTES // 8 * 7

CHAIN_ROWS = V7X_MXU_DIM
QKV_B_TM = 1024
XN_PROJ_TM, XN_PROJ_TN = 2048, 1024
GQA_TQ = 2048
GQA_CHAIN_ROWS = V7X_MXU_DIM // 2
MIX1_TM, MIX1_TN = 1024, 512
MIX2_TM = 512
FFN_TM, FFN_TF = 1024, 512
PLE_TM = 512


def _exact_div(a, b):
    assert a % b == 0, (a, b)
    return a // b


def _rms(x, g):
    ms = jnp.mean(x * x, axis=-1, keepdims=True)
    return x * lax.rsqrt(ms + RMS_EPS) * g


def _sigmoid(x):
    return 0.5 * jnp.tanh(0.5 * x) + 0.5


def _resident(shape):
    return pl.BlockSpec(shape, lambda *_: (0,) * len(shape), pipeline_mode=pl.Buffered(1))


def _rope(a, cos, sin_lo, sin_hi):
    return a * cos + pltpu.roll(a, 96, 1) * sin_lo + pltpu.roll(a, 32, 1) * sin_hi


def _row_chunks(n_rows):
    return [slice(r, r + CHAIN_ROWS) for r in range(0, n_rows, CHAIN_ROWS)]


def _tail_chunks(n_rows):
    last = CHAIN_ROWS // 2
    return [slice(0, n_rows - last), slice(n_rows - last, n_rows)]


def _qkv_b_kernel(x_ref, g_ref, w_ref, cos_ref, slo_ref, shi_ref, gq_ref, gk_ref,
                  xn_ref, o_ref):
    for rows in _row_chunks(x_ref.shape[0]):
        xn = _rms(x_ref[rows, :], g_ref[...]).astype(BF16)
        xn_ref[rows, :] = xn
        acc = jnp.dot(xn, w_ref[...], preferred_element_type=F32)
        cos, slo, shi = cos_ref[rows, :], slo_ref[rows, :], shi_ref[rows, :]
        for h in range(GQA_Q_HEADS + GQA_KV_HEADS):
            sl = slice(h * HEAD_DIM, (h + 1) * HEAD_DIM)
            if h < GQA_Q_HEADS:
                a = _rope(_rms(acc[:, sl], gq_ref[...]), cos, slo, shi) * Q_SCALE
            else:
                a = _rope(_rms(acc[:, sl], gk_ref[...]), cos, slo, shi)
            o_ref[rows, sl] = a.astype(BF16)
        v0 = GQA_Q_WIDTH + GQA_KV_WIDTH
        o_ref[rows, v0:] = acc[:, v0:].astype(BF16)


def _qkv_b(x2d, g_pre, w_in, cos, sin_lo, sin_hi, g_q, g_k, seq):
    t, d = x2d.shape
    tm = QKV_B_TM
    n_pos_blocks = _exact_div(seq, tm)
    pos_spec = pl.BlockSpec((tm, HEAD_DIM), lambda i: (i % n_pos_blocks, 0))
    return pl.pallas_call(
        _qkv_b_kernel,
        out_shape=(jax.ShapeDtypeStruct((t, d), BF16), jax.ShapeDtypeStruct((t, N_ATTN_B), BF16)),
        grid=(_exact_div(t, tm),),
        in_specs=[
            pl.BlockSpec((tm, d), lambda i: (i, 0)),
            _resident((1, d)),
            pl.BlockSpec((d, N_ATTN_B), lambda i: (0, _exact_div(N_ATTN_A, N_ATTN_B)),
                         pipeline_mode=pl.Buffered(1)),
            pos_spec, pos_spec, pos_spec,
            _resident((1, HEAD_DIM)),
            _resident((1, HEAD_DIM)),
        ],
        out_specs=(pl.BlockSpec((tm, d), lambda i: (i, 0)),
                   pl.BlockSpec((tm, N_ATTN_B), lambda i: (i, 0))),
        compiler_params=pltpu.CompilerParams(
            dimension_semantics=("parallel",),
            vmem_limit_bytes=VMEM_LIMIT),
        name="qkv_b",
    )(x2d, g_pre, w_in, cos, sin_lo, sin_hi, g_q, g_k)


def _colscale_kernel(xn_ref, w_ref, cs_ref, o_ref):
    acc = jnp.dot(xn_ref[...], w_ref[...], preferred_element_type=F32)
    o_ref[...] = (acc * cs_ref[...]).astype(BF16)


def _xn_proj(body, name, xn, w, n, *extra):
    t, d = xn.shape
    tm, tn = XN_PROJ_TM, XN_PROJ_TN
    extra_specs = [pl.BlockSpec((1, tn), lambda i, j: (0, j)) for _ in extra]
    return pl.pallas_call(
        body,
        out_shape=jax.ShapeDtypeStruct((t, n), BF16),
        grid=(_exact_div(t, tm), _exact_div(n, tn)),
        in_specs=[
            pl.BlockSpec((tm, d), lambda i, j: (i, 0)),
            pl.BlockSpec((d, tn), lambda i, j: (0, j)),
        ] + extra_specs,
        out_specs=pl.BlockSpec((tm, tn), lambda i, j: (i, j)),
        compiler_params=pltpu.CompilerParams(
            dimension_semantics=("parallel", "arbitrary"),
            vmem_limit_bytes=VMEM_LIMIT),
        name=name,
    )(xn, w, *extra)


def _na_window_start(blk, rows):
    return jnp.clip(blk * NA_QROWS - NA_ROWS // 2, 0, rows - NA_WIN_ROWS)


def _na_kernel(q_ref, k_ref, v_ref, tab_ref, o_ref, *, rows):
    n_blk = rows // NA_QROWS
    ones = jnp.ones((NA_TK, HEAD_DIM), BF16)
    for i in range(NA_BLOCKS_PER_STEP):
        blk = pl.program_id(1) * NA_BLOCKS_PER_STEP + i
        qrows = slice(i * NA_TQ, (i + 1) * NA_TQ)
        k0 = pl.multiple_of(_na_window_start(blk, rows) * GRID_W, GRID_W)
        variant = jnp.where(blk == 0, 0, jnp.where(blk == n_blk - 1, 2, 1))
        for h in range(NA_HEADS):
            sl = slice(h * HEAD_DIM, (h + 1) * HEAD_DIM)
            q = q_ref[qrows, sl]
            k = k_ref[pl.ds(k0, NA_TK), sl]
            v = v_ref[pl.ds(k0, NA_TK), sl]
            s = lax.dot_general(q, k, (((1,), (1,)), ((), ())), preferred_element_type=F32)
            s = s + tab_ref[variant, h]
            m = jnp.max(s, axis=-1, keepdims=True)
            p = jnp.exp2(s - m).astype(BF16)
            ol = jnp.dot(p, jnp.concatenate([v, ones], axis=1), preferred_element_type=F32)
            o_ref[qrows, sl] = (ol[:, :HEAD_DIM] * (1.0 / ol[:, HEAD_DIM:])).astype(BF16)


def _na_bias_tables(rpb, rows):
    n_blk = rows // NA_QROWS
    qc = np.arange(GRID_W)[:, None]
    kc = np.arange(GRID_W)[None, :]
    dc = np.clip(kc - qc, -(NA_COLS - 1), NA_COLS - 1) + NA_COLS - 1
    onehot = (dc[None] == np.arange(2 * NA_COLS - 1)[:, None, None]).astype(np.float32)
    cexp = jnp.einsum("hrd,dqk->hqrk", rpb.astype(F32) * LOG2E, jnp.asarray(onehot),
                      precision=lax.Precision.HIGHEST)
    pad = NA_WIN_ROWS
    cexp = jnp.pad(cexp, ((0, 0), (0, 0), (pad, pad), (0, 0)))
    cs = np.clip(qc - NA_COLS // 2, 0, GRID_W - NA_COLS)
    col_ok = (kc >= cs) & (kc < cs + NA_COLS)
    tabs = []
    for blk in (0, 1, n_blk - 1):
        r0 = blk * NA_QROWS
        k0 = int(np.clip(r0 - NA_ROWS // 2, 0, rows - NA_WIN_ROWS))
        per_row = []
        for qr in range(r0, r0 + NA_QROWS):
            start = k0 - qr + NA_ROWS - 1 + pad
            per_row.append(cexp[:, :, start:start + NA_WIN_ROWS])
        b = jnp.stack(per_row, axis=1)
        qr_ = (r0 + np.arange(NA_QROWS))[:, None]
        kr_ = (k0 + np.arange(NA_WIN_ROWS))[None, :]
        rs = np.clip(qr_ - NA_ROWS // 2, 0, rows - NA_ROWS)
        row_ok = (kr_ >= rs) & (kr_ < rs + NA_ROWS)
        ok = row_ok[:, None, :, None] & col_ok[None, :, None, :]
        b = jnp.where(jnp.asarray(ok)[None], b, NEG_INF)
        tabs.append(b.reshape(NA_HEADS, NA_TQ, NA_TK))
    return jnp.stack(tabs)


def _na_attention(pa, tables, batch, seq):
    rows = seq // GRID_W
    n_steps = _exact_div(rows, NA_QROWS * NA_BLOCKS_PER_STEP)
    tq = NA_TQ * NA_BLOCKS_PER_STEP
    t = batch * seq

    return pl.pallas_call(
        functools.partial(_na_kernel, rows=rows),
        out_shape=jax.ShapeDtypeStruct((t, NA_WIDTH), BF16),
        grid=(batch, n_steps),
        in_specs=[
            pl.BlockSpec((tq, NA_WIDTH), lambda b, i: (b * n_steps + i, 0)),
            pl.BlockSpec((seq, NA_WIDTH), lambda b, i: (b, 1)),
            pl.BlockSpec((seq, NA_WIDTH), lambda b, i: (b, 2)),
            _resident(tables.shape),
        ],
        out_specs=pl.BlockSpec((tq, NA_WIDTH), lambda b, i: (b * n_steps + i, 0)),
        compiler_params=pltpu.CompilerParams(
            dimension_semantics=("parallel", "arbitrary"),
            vmem_limit_bytes=VMEM_LIMIT),
        name="na_attn",
    )(pa, pa, pa, tables)


def _gqa_kernel(q_ref, k_ref, v_ref, o_ref, vaug_ref):
    @pl.when(pl.program_id(2) == 0)
    def _():
        vaug_ref[:, :HEAD_DIM] = v_ref[...]
        vaug_ref[:, HEAD_DIM:] = jnp.ones((v_ref.shape[0], HEAD_DIM), BF16)

    k = k_ref[...]
    for r in range(GQA_TQ // GQA_CHAIN_ROWS):
        rows = slice(r * GQA_CHAIN_ROWS, (r + 1) * GQA_CHAIN_ROWS)
        for h in range(GQA_GROUP):
            sl = slice(h * HEAD_DIM, (h + 1) * HEAD_DIM)
            s = lax.dot_general(q_ref[rows, sl], k, (((1,), (1,)), ((), ())),
                                preferred_element_type=F32)
            m = jnp.max(s, axis=-1, keepdims=True)
            p = jnp.exp2(s - m).astype(BF16)
            ol = jnp.dot(p, vaug_ref[...], preferred_element_type=F32)
            o_ref[rows, sl] = (ol[:, :HEAD_DIM] * (1.0 / ol[:, HEAD_DIM:])).astype(BF16)


def _gqa_attention(pb, batch, seq):
    t = batch * seq
    n_q = _exact_div(seq, GQA_TQ)
    gw = GQA_GROUP * HEAD_DIM
    return pl.pallas_call(
        _gqa_kernel,
        out_shape=jax.ShapeDtypeStruct((t, GQA_Q_WIDTH), BF16),
        grid=(batch, GQA_KV_HEADS, n_q),
        in_specs=[
            pl.BlockSpec((GQA_TQ, gw), lambda b, g, i: (b * n_q + i, g)),
            pl.BlockSpec((seq, HEAD_DIM), lambda b, g, i: (b, GQA_Q_WIDTH // HEAD_DIM + g)),
            pl.BlockSpec((seq, HEAD_DIM),
                         lambda b, g, i: (b, (GQA_Q_WIDTH + GQA_KV_WIDTH) // HEAD_DIM + g)),
        ],
        out_specs=pl.BlockSpec((GQA_TQ, gw), lambda b, g, i: (b * n_q + i, g)),
        scratch_shapes=[pltpu.VMEM((seq, 2 * HEAD_DIM), BF16)],
        compiler_params=pltpu.CompilerParams(
            dimension_semantics=("parallel", "parallel", "arbitrary"),
            vmem_limit_bytes=VMEM_LIMIT),
        name="gqa_attn",
    )(pb, pb, pb)


def _mix1_kernel(xn_ref, a_ref, b_ref, wga_ref, wgb_ref, wa_ref, wb_ref, o_ref):
    xn = xn_ref[...]
    ga = _sigmoid(jnp.dot(xn, wga_ref[...], preferred_element_type=F32))
    ya = jnp.dot(a_ref[...], wa_ref[...], preferred_element_type=F32)
    gb = _sigmoid(jnp.dot(xn, wgb_ref[...], preferred_element_type=F32))
    yb = jnp.dot(b_ref[...], wb_ref[...], preferred_element_type=F32)
    o_ref[...] = (ga * ya + gb * yb).astype(BF16)


def _mix1(xn, na_out, gqa_out, w_in, wa, wb):
    t, d = xn.shape
    tm, tn = MIX1_TM, MIX1_TN
    n_j = _exact_div(d, tn)
    j0 = _exact_div(N_ATTN_A + N_ATTN_B, tn)
    return pl.pallas_call(
        _mix1_kernel,
        out_shape=jax.ShapeDtypeStruct((t, d), BF16),
        grid=(_exact_div(t, tm), n_j),
        in_specs=[
            pl.BlockSpec((tm, d), lambda i, j: (i, 0)),
            pl.BlockSpec((tm, NA_WIDTH), lambda i, j: (i, 0)),
            pl.BlockSpec((tm, GQA_Q_WIDTH), lambda i, j: (i, 0)),
            pl.BlockSpec((d, tn), lambda i, j: (0, j0 + j)),
            pl.BlockSpec((d, tn), lambda i, j: (0, j0 + n_j + j)),
            pl.BlockSpec((NA_WIDTH, tn), lambda i, j: (0, j)),
            pl.BlockSpec((GQA_Q_WIDTH, tn), lambda i, j: (0, j)),
        ],
        out_specs=pl.BlockSpec((tm, tn), lambda i, j: (i, j)),
        compiler_params=pltpu.CompilerParams(
            dimension_semantics=("parallel", "arbitrary"),
            vmem_limit_bytes=VMEM_LIMIT),
        name="mix1",
    )(xn, na_out, gqa_out, w_in, w_in, wa, wb)


def _mix2_kernel(m_ref, x_ref, wo_ref, gpost_ref, gpre_ref, h_ref, xn_ref):
    for rows in _tail_chunks(m_ref.shape[0]):
        m2 = jnp.dot(m_ref[rows, :], wo_ref[...], preferred_element_type=F32)
        h = x_ref[rows, :] + _rms(m2, gpost_ref[...])
        h_ref[rows, :] = h
        xn_ref[rows, :] = _rms(h, gpre_ref[...]).astype(BF16)


def _mix2(mixed, x2d, wo, g_post, g_pre_ffn):
    t, d = x2d.shape
    tm = MIX2_TM
    row_spec = pl.BlockSpec((tm, d), lambda i: (i, 0))
    return pl.pallas_call(
        _mix2_kernel,
        out_shape=(jax.ShapeDtypeStruct((t, d), F32), jax.ShapeDtypeStruct((t, d), BF16)),
        grid=(_exact_div(t, tm),),
        in_specs=[row_spec, row_spec, _resident(wo.shape), _resident((1, d)), _resident((1, d))],
        out_specs=(row_spec, row_spec),
        compiler_params=pltpu.CompilerParams(
            dimension_semantics=("parallel",),
            vmem_limit_bytes=VMEM_LIMIT),
        name="mix2",
    )(mixed, x2d, wo, g_post, g_pre_ffn)


def _ffn_kernel(xn_ref, wg_ref, wu_ref, wo_ref, f_ref, acc_ref):
    j = pl.program_id(1)

    @pl.when((pl.program_id(0) == 0) & (j == 0))
    def _():
        acc_ref[...] = jnp.zeros_like(acc_ref)

    xn = xn_ref[...]
    g = jnp.dot(xn, wg_ref[...], preferred_element_type=F32)
    u = jnp.dot(xn, wu_ref[...], preferred_element_type=F32)
    act = (g * _sigmoid(g) * u).astype(BF16)
    prev = jnp.where(j == 0, 0.0, acc_ref[...])
    acc = prev + jnp.dot(act, wo_ref[...], preferred_element_type=F32)
    acc_ref[...] = acc
    f_ref[...] = acc.astype(BF16)


def _ffn(xn2, w_ffn_in, w_ffn_out):
    t, d = xn2.shape
    d_ff = w_ffn_out.shape[0]
    tm = FFN_TM
    n_f = _exact_div(d_ff, FFN_TF)
    return pl.pallas_call(
        _ffn_kernel,
        out_shape=jax.ShapeDtypeStruct((t, d), BF16),
        grid=(_exact_div(t, tm), n_f),
        in_specs=[
            pl.BlockSpec((tm, d), lambda i, j: (i, 0)),
            pl.BlockSpec((d, FFN_TF), lambda i, j: (0, j)),
            pl.BlockSpec((d, FFN_TF), lambda i, j: (0, n_f + j)),
            pl.BlockSpec((FFN_TF, d), lambda i, j: (j, 0)),
        ],
        out_specs=pl.BlockSpec((tm, d), lambda i, j: (i, 0)),
        scratch_shapes=[pltpu.VMEM((tm, d), F32)],
        compiler_params=pltpu.CompilerParams(
            dimension_semantics=("arbitrary", "arbitrary"),
            vmem_limit_bytes=VMEM_LIMIT),
        name="ffn",
    )(xn2, w_ffn_in, w_ffn_in, w_ffn_out)


def _ple_kernel(h_ref, f_ref, p_ref, wple_ref, wgate_ref, gffn_ref, gple_ref, y_ref):
    for rows in _row_chunks(h_ref.shape[0]):
        h = h_ref[rows, :] + _rms(f_ref[rows, :].astype(F32), gffn_ref[...])
        e = jnp.dot(p_ref[rows, :].astype(BF16), wple_ref[...], preferred_element_type=F32)
        gate = _sigmoid(jnp.dot(h.astype(BF16), wgate_ref[...], preferred_element_type=F32))
        y_ref[rows, :] = h + _rms(gate * e, gple_ref[...])


def _ple(h1, f, p2d, w_ple, w_gate, g_ffn, g_ple):
    t, d = h1.shape
    pd = p2d.shape[1]
    tm = PLE_TM
    return pl.pallas_call(
        _ple_kernel,
        out_shape=jax.ShapeDtypeStruct((t, d), F32),
        grid=(_exact_div(t, tm),),
        in_specs=[
            pl.BlockSpec((tm, d), lambda i: (i, 0)),
            pl.BlockSpec((tm, d), lambda i: (i, 0)),
            pl.BlockSpec((tm, pd), lambda i: (i, 0)),
            _resident(w_ple.shape),
            _resident(w_gate.shape),
            _resident((1, d)),
            _resident((1, d)),
        ],
        out_specs=pl.BlockSpec((tm, d), lambda i: (i, 0)),
        compiler_params=pltpu.CompilerParams(
            dimension_semantics=("parallel",),
            vmem_limit_bytes=VMEM_LIMIT),
        name="ple",
    )(h1, f, p2d, w_ple, w_gate, g_ffn, g_ple)


def _rope_tables(seq):
    half = HEAD_DIM // 2
    t = np.arange(seq)
    inv_freq = ROPE_THETA ** (-np.arange(0, half, 2, dtype=np.float64) / half)

    def tables(pos):
        ang = pos.astype(np.float64)[:, None] * inv_freq[None, :]
        return (np.concatenate([np.cos(ang), np.cos(ang)], axis=-1),
                np.concatenate([np.sin(ang), np.sin(ang)], axis=-1))

    cos_r, sin_r = tables(t // GRID_W)
    cos_c, sin_c = tables(t % GRID_W)
    cos = np.concatenate([cos_r, cos_c], axis=-1)
    sin = np.concatenate([sin_r, sin_c], axis=-1)
    first = (np.arange(HEAD_DIM) % half) < half // 2
    sin_lo = np.where(first[None, :], -sin, 0.0)
    sin_hi = np.where(first[None, :], 0.0, sin)
    return tuple(jnp.asarray(a, F32) for a in (cos, sin_lo, sin_hi))


def _layer(x, p, w, seq):
    batch = x.shape[0]
    d = x.shape[-1]
    x2d = x.reshape(batch * seq, d)
    p2d = p.reshape(batch * seq, p.shape[-1])
    xn, pb = _qkv_b(x2d, w["g_pre_mix"], w["w_in"], w["cos"], w["sin_lo"], w["sin_hi"],
                    w["g_q"], w["g_k"], seq)
    pa = _xn_proj(_colscale_kernel, "qkv_a", xn, w["w_in"], N_ATTN_A, w["qkv_a_scale"])
    na_out = _na_attention(pa, w["na_tables"], batch, seq)
    gqa_out = _gqa_attention(pb, batch, seq)
    mixed = _mix1(xn, na_out, gqa_out, w["w_in"], w["w_branch_a"], w["w_branch_b"])
    h1, xn2 = _mix2(mixed, x2d, w["w_out"], w["g_post_mix"], w["g_pre_ffn"])
    f = _ffn(xn2, w["w_ffn_in"], w["w_ffn_out"])
    y = _ple(h1, f, p2d, w["w_ple"], w["w_ple_gate"], w["g_post_ffn"], w["g_ple"])
    return y.reshape(x.shape)


def kernel(x_prompt, x_sample, p_prompt, p_sample, g_pre_mix, w_in, rpb, g_q, g_k, w_branch_a, w_branch_b, w_out, g_post_mix, g_pre_ffn, w_ffn_in, w_ffn_out, g_post_ffn, w_ple, w_ple_gate, g_ple):
    depth = w_in.shape[0]
    h_p, h_s = x_prompt, x_sample
    qkv_a_scale = jnp.concatenate(
        [jnp.full((1, NA_WIDTH), Q_SCALE, F32), jnp.ones((1, 2 * NA_WIDTH), F32)], axis=1)
    for i in range(depth):
        w = {
            "w_in": w_in[i].astype(BF16),
            "qkv_a_scale": qkv_a_scale,
            "w_branch_a": w_branch_a[i].astype(BF16),
            "w_branch_b": w_branch_b[i].astype(BF16),
            "w_out": w_out[i].astype(BF16),
            "w_ffn_in": w_ffn_in[i].astype(BF16),
            "w_ffn_out": w_ffn_out[i].astype(BF16),
            "w_ple": w_ple[i].astype(BF16),
            "w_ple_gate": w_ple_gate[i].astype(BF16),
            "g_pre_mix": g_pre_mix[i][None, :],
            "g_q": g_q[i][None, :],
            "g_k": g_k[i][None, :],
            "g_post_mix": g_post_mix[i][None, :],
            "g_pre_ffn": g_pre_ffn[i][None, :],
            "g_post_ffn": g_post_ffn[i][None, :],
            "g_ple": g_ple[i][None, :],
        }
        outs = []
        for h, p in ((h_p, p_prompt[i]), (h_s, p_sample[i])):
            seq = h.shape[1]
            w["cos"], w["sin_lo"], w["sin_hi"] = _rope_tables(seq)
            w["na_tables"] = _na_bias_tables(rpb[i], seq // GRID_W)
            outs.append(_layer(h, p, w, seq))
        h_p, h_s = outs
    return (h_p, h_s)
```

```python
import functools
import math

import numpy as np
import jax
import jax.numpy as jnp
from jax import lax
from jax.experimental import pallas as pl
from jax.experimental.pallas import tpu as pltpu

F32 = jnp.float32
BF16 = jnp.bfloat16

HEAD_DIM = 128
NA_HEADS = 8
GQA_Q_HEADS = 8
GQA_KV_HEADS = 2
GQA_GROUP = GQA_Q_HEADS // GQA_KV_HEADS
GRID_W = 64
NA_ROWS = 8
NA_COLS = 16
ROPE_THETA = 10000.0
RMS_EPS = 1e-6
NEG_INF = -1e30
LOG2E = math.log2(math.e)
Q_SCALE = LOG2E / math.sqrt(HEAD_DIM)

NA_WIDTH = NA_HEADS * HEAD_DIM
GQA_Q_WIDTH = GQA_Q_HEADS * HEAD_DIM
GQA_KV_WIDTH = GQA_KV_HEADS * HEAD_DIM
N_ATTN_A = 3 * NA_WIDTH
N_ATTN_B = GQA_Q_WIDTH + 2 * GQA_KV_WIDTH

NA_QROWS = 4
NA_WIN_ROWS = 12
NA_TQ = NA_QROWS * GRID_W
NA_TK = NA_WIN_ROWS * GRID_W
NA_BLOCKS_PER_STEP = 8

V7X_VMEM_BYTES = 64 * 1024 * 1024
V7X_MXU_DIM = 256
VMEM_LIMIT = V7X_VMEM_BYTES // 8 * 7

CHAIN_ROWS = V7X_MXU_DIM
QKV_B_TM = 1024
XN_PROJ_TM, XN_PROJ_TN = 2048, 1024
GQA_TQ = 2048
GQA_CHAIN_ROWS = V7X_MXU_DIM // 2
MIX1_TM, MIX1_TN = 1024, 512
MIX2_TM = 512
FFN_TM, FFN_TF = 1024, 512
PLE_TM = 512


def _exact_div(a, b):
    assert a % b == 0, (a, b)
    return a // b


def _rms(x, g):
    ms = jnp.mean(x * x, axis=-1, keepdims=True)
    return x * lax.rsqrt(ms + RMS_EPS) * g


def _sigmoid(x):
    return 0.5 * jnp.tanh(0.5 * x) + 0.5


def _resident(shape):
    return pl.BlockSpec(shape, lambda *_: (0,) * len(shape), pipeline_mode=pl.Buffered(1))


def _rope(a, cos, sin_lo, sin_hi):
    return a * cos + pltpu.roll(a, 96, 1) * sin_lo + pltpu.roll(a, 32, 1) * sin_hi


def _row_chunks(n_rows):
    return [slice(r, r + CHAIN_ROWS) for r in range(0, n_rows, CHAIN_ROWS)]


def _tail_chunks(n_rows):
    last = CHAIN_ROWS // 2
    return [slice(0, n_rows - last), slice(n_rows - last, n_rows)]


def _qkv_b_kernel(x_ref, g_ref, w_ref, cos_ref, slo_ref, shi_ref, gq_ref, gk_ref,
                  xn_ref, o_ref):
    for rows in _row_chunks(x_ref.shape[0]):
        xn = _rms(x_ref[rows, :], g_ref[...]).astype(BF16)
        xn_ref[rows, :] = xn
        acc = jnp.dot(xn, w_ref[...], preferred_element_type=F32)
        cos, slo, shi = cos_ref[rows, :], slo_ref[rows, :], shi_ref[rows, :]
        for h in range(GQA_Q_HEADS + GQA_KV_HEADS):
            sl = slice(h * HEAD_DIM, (h + 1) * HEAD_DIM)
            if h < GQA_Q_HEADS:
                a = _rope(_rms(acc[:, sl], gq_ref[...]), cos, slo, shi) * Q_SCALE
            else:
                a = _rope(_rms(acc[:, sl], gk_ref[...]), cos, slo, shi)
            o_ref[rows, sl] = a.astype(BF16)
        v0 = GQA_Q_WIDTH + GQA_KV_WIDTH
        o_ref[rows, v0:] = acc[:, v0:].astype(BF16)


def _qkv_b(x2d, g_pre, w_in, cos, sin_lo, sin_hi, g_q, g_k, seq):
    t, d = x2d.shape
    tm = QKV_B_TM
    n_pos_blocks = _exact_div(seq, tm)
    pos_spec = pl.BlockSpec((tm, HEAD_DIM), lambda i: (i % n_pos_blocks, 0))
    return pl.pallas_call(
        _qkv_b_kernel,
        out_shape=(jax.ShapeDtypeStruct((t, d), BF16), jax.ShapeDtypeStruct((t, N_ATTN_B), BF16)),
        grid=(_exact_div(t, tm),),
        in_specs=[
            pl.BlockSpec((tm, d), lambda i: (i, 0)),
            _resident((1, d)),
            pl.BlockSpec((d, N_ATTN_B), lambda i: (0, _exact_div(N_ATTN_A, N_ATTN_B)),
                         pipeline_mode=pl.Buffered(1)),
            pos_spec, pos_spec, pos_spec,
            _resident((1, HEAD_DIM)),
            _resident((1, HEAD_DIM)),
        ],
        out_specs=(pl.BlockSpec((tm, d), lambda i: (i, 0)),
                   pl.BlockSpec((tm, N_ATTN_B), lambda i: (i, 0))),
        compiler_params=pltpu.CompilerParams(
            dimension_semantics=("parallel",),
            vmem_limit_bytes=VMEM_LIMIT),
        name="qkv_b",
    )(x2d, g_pre, w_in, cos, sin_lo, sin_hi, g_q, g_k)


def _colscale_kernel(xn_ref, w_ref, cs_ref, o_ref):
    acc = jnp.dot(xn_ref[...], w_ref[...], preferred_element_type=F32)
    o_ref[...] = (acc * cs_ref[...]).astype(BF16)


def _xn_proj(body, name, xn, w, n, *extra):
    t, d = xn.shape
    tm, tn = XN_PROJ_TM, XN_PROJ_TN
    extra_specs = [pl.BlockSpec((1, tn), lambda i, j: (0, j)) for _ in extra]
    return pl.pallas_call(
        body,
        out_shape=jax.ShapeDtypeStruct((t, n), BF16),
        grid=(_exact_div(t, tm), _exact_div(n, tn)),
        in_specs=[
            pl.BlockSpec((tm, d), lambda i, j: (i, 0)),
            pl.BlockSpec((d, tn), lambda i, j: (0, j)),
        ] + extra_specs,
        out_specs=pl.BlockSpec((tm, tn), lambda i, j: (i, j)),
        compiler_params=pltpu.CompilerParams(
            dimension_semantics=("parallel", "arbitrary"),
            vmem_limit_bytes=VMEM_LIMIT),
        name=name,
    )(xn, w, *extra)


def _na_window_start(blk, rows):
    return jnp.clip(blk * NA_QROWS - NA_ROWS // 2, 0, rows - NA_WIN_ROWS)


def _na_kernel(q_ref, k_ref, v_ref, tab_ref, o_ref, *, rows):
    n_blk = rows // NA_QROWS
    ones = jnp.ones((NA_TK, HEAD_DIM), BF16)
    for i in range(NA_BLOCKS_PER_STEP):
        blk = pl.program_id(1) * NA_BLOCKS_PER_STEP + i
        qrows = slice(i * NA_TQ, (i + 1) * NA_TQ)
        k0 = pl.multiple_of(_na_window_start(blk, rows) * GRID_W, GRID_W)
        variant = jnp.where(blk == 0, 0, jnp.where(blk == n_blk - 1, 2, 1))
        for h in range(NA_HEADS):
            sl = slice(h * HEAD_DIM, (h + 1) * HEAD_DIM)
            q = q_ref[qrows, sl]
            k = k_ref[pl.ds(k0, NA_TK), sl]
            v = v_ref[pl.ds(k0, NA_TK), sl]
            s = lax.dot_general(q, k, (((1,), (1,)), ((), ())), preferred_element_type=F32)
            s = s + tab_ref[variant, h]
            m = jnp.max(s, axis=-1, keepdims=True)
            p = jnp.exp2(s - m).astype(BF16)
            ol = jnp.dot(p, jnp.concatenate([v, ones], axis=1), preferred_element_type=F32)
            o_ref[qrows, sl] = (ol[:, :HEAD_DIM] * (1.0 / ol[:, HEAD_DIM:])).astype(BF16)


def _na_bias_tables(rpb, rows):
    n_blk = rows // NA_QROWS
    qc = np.arange(GRID_W)[:, None]
    kc = np.arange(GRID_W)[None, :]
    dc = np.clip(kc - qc, -(NA_COLS - 1), NA_COLS - 1) + NA_COLS - 1
    onehot = (dc[None] == np.arange(2 * NA_COLS - 1)[:, None, None]).astype(np.float32)
    cexp = jnp.einsum("hrd,dqk->hqrk", rpb.astype(F32) * LOG2E, jnp.asarray(onehot),
                      precision=lax.Precision.HIGHEST)
    pad = NA_WIN_ROWS
    cexp = jnp.pad(cexp, ((0, 0), (0, 0), (pad, pad), (0, 0)))
    cs = np.clip(qc - NA_COLS // 2, 0, GRID_W - NA_COLS)
    col_ok = (kc >= cs) & (kc < cs + NA_COLS)
    tabs = []
    for blk in (0, 1, n_blk - 1):
        r0 = blk * NA_QROWS
        k0 = int(np.clip(r0 - NA_ROWS // 2, 0, rows - NA_WIN_ROWS))
        per_row = []
        for qr in range(r0, r0 + NA_QROWS):
            start = k0 - qr + NA_ROWS - 1 + pad
            per_row.append(cexp[:, :, start:start + NA_WIN_ROWS])
        b = jnp.stack(per_row, axis=1)
        qr_ = (r0 + np.arange(NA_QROWS))[:, None]
        kr_ = (k0 + np.arange(NA_WIN_ROWS))[None, :]
        rs = np.clip(qr_ - NA_ROWS // 2, 0, rows - NA_ROWS)
        row_ok = (kr_ >= rs) & (kr_ < rs + NA_ROWS)
        ok = row_ok[:, None, :, None] & col_ok[None, :, None, :]
        b = jnp.where(jnp.asarray(ok)[None], b, NEG_INF)
        tabs.append(b.reshape(NA_HEADS, NA_TQ, NA_TK))
    return jnp.stack(tabs)


def _na_attention(pa, tables, batch, seq):
    rows = seq // GRID_W
    n_steps = _exact_div(rows, NA_QROWS * NA_BLOCKS_PER_STEP)
    tq = NA_TQ * NA_BLOCKS_PER_STEP
    t = batch * seq

    return pl.pallas_call(
        functools.partial(_na_kernel, rows=rows),
        out_shape=jax.ShapeDtypeStruct((t, NA_WIDTH), BF16),
        grid=(batch, n_steps),
        in_specs=[
            pl.BlockSpec((tq, NA_WIDTH), lambda b, i: (b * n_steps + i, 0)),
            pl.BlockSpec((seq, NA_WIDTH), lambda b, i: (b, 1)),
            pl.BlockSpec((seq, NA_WIDTH), lambda b, i: (b, 2)),
            _resident(tables.shape),
        ],
        out_specs=pl.BlockSpec((tq, NA_WIDTH), lambda b, i: (b * n_steps + i, 0)),
        compiler_params=pltpu.CompilerParams(
            dimension_semantics=("parallel", "arbitrary"),
            vmem_limit_bytes=VMEM_LIMIT),
        name="na_attn",
    )(pa, pa, pa, tables)


def _gqa_kernel(q_ref, k_ref, v_ref, o_ref, vaug_ref):
    @pl.when(pl.program_id(2) == 0)
    def _():
        vaug_ref[:, :HEAD_DIM] = v_ref[...]
        vaug_ref[:, HEAD_DIM:] = jnp.ones((v_ref.shape[0], HEAD_DIM), BF16)

    k = k_ref[...]
    for r in range(GQA_TQ // GQA_CHAIN_ROWS):
        rows = slice(r * GQA_CHAIN_ROWS, (r + 1) * GQA_CHAIN_ROWS)
        for h in range(GQA_GROUP):
            sl = slice(h * HEAD_DIM, (h + 1) * HEAD_DIM)
            s = lax.dot_general(q_ref[rows, sl], k, (((1,), (1,)), ((), ())),
                                preferred_element_type=F32)
            m = jnp.max(s, axis=-1, keepdims=True)
            p = jnp.exp2(s - m).astype(BF16)
            ol = jnp.dot(p, vaug_ref[...], preferred_element_type=F32)
            o_ref[rows, sl] = (ol[:, :HEAD_DIM] * (1.0 / ol[:, HEAD_DIM:])).astype(BF16)


def _gqa_attention(pb, batch, seq):
    t = batch * seq
    n_q = _exact_div(seq, GQA_TQ)
    gw = GQA_GROUP * HEAD_DIM
    return pl.pallas_call(
        _gqa_kernel,
        out_shape=jax.ShapeDtypeStruct((t, GQA_Q_WIDTH), BF16),
        grid=(batch, GQA_KV_HEADS, n_q),
        in_specs=[
            pl.BlockSpec((GQA_TQ, gw), lambda b, g, i: (b * n_q + i, g)),
            pl.BlockSpec((seq, HEAD_DIM), lambda b, g, i: (b, GQA_Q_WIDTH // HEAD_DIM + g)),
            pl.BlockSpec((seq, HEAD_DIM),
                         lambda b, g, i: (b, (GQA_Q_WIDTH + GQA_KV_WIDTH) // HEAD_DIM + g)),
        ],
        out_specs=pl.BlockSpec((GQA_TQ, gw), lambda b, g, i: (b * n_q + i, g)),
        scratch_shapes=[pltpu.VMEM((seq, 2 * HEAD_DIM), BF16)],
        compiler_params=pltpu.CompilerParams(
            dimension_semantics=("parallel", "parallel", "arbitrary"),
            vmem_limit_bytes=VMEM_LIMIT),
        name="gqa_attn",
    )(pb, pb, pb)


def _mix1_kernel(xn_ref, a_ref, b_ref, wga_ref, wgb_ref, wa_ref, wb_ref, o_ref):
    xn = xn_ref[...]
    ga = _sigmoid(jnp.dot(xn, wga_ref[...], preferred_element_type=F32))
    ya = jnp.dot(a_ref[...], wa_ref[...], preferred_element_type=F32)
    gb = _sigmoid(jnp.dot(xn, wgb_ref[...], preferred_element_type=F32))
    yb = jnp.dot(b_ref[...], wb_ref[...], preferred_element_type=F32)
    o_ref[...] = (ga * ya + gb * yb).astype(BF16)


def _mix1(xn, na_out, gqa_out, w_in, wa, wb):
    t, d = xn.shape
    tm, tn = MIX1_TM, MIX1_TN
    n_j = _exact_div(d, tn)
    j0 = _exact_div(N_ATTN_A + N_ATTN_B, tn)
    return pl.pallas_call(
        _mix1_kernel,
        out_shape=jax.ShapeDtypeStruct((t, d), BF16),
        grid=(_exact_div(t, tm), n_j),
        in_specs=[
            pl.BlockSpec((tm, d), lambda i, j: (i, 0)),
            pl.BlockSpec((tm, NA_WIDTH), lambda i, j: (i, 0)),
            pl.BlockSpec((tm, GQA_Q_WIDTH), lambda i, j: (i, 0)),
            pl.BlockSpec((d, tn), lambda i, j: (0, j0 + j)),
            pl.BlockSpec((d, tn), lambda i, j: (0, j0 + n_j + j)),
            pl.BlockSpec((NA_WIDTH, tn), lambda i, j: (0, j)),
            pl.BlockSpec((GQA_Q_WIDTH, tn), lambda i, j: (0, j)),
        ],
        out_specs=pl.BlockSpec((tm, tn), lambda i, j: (i, j)),
        compiler_params=pltpu.CompilerParams(
            dimension_semantics=("parallel", "arbitrary"),
            vmem_limit_bytes=VMEM_LIMIT),
        name="mix1",
    )(xn, na_out, gqa_out, w_in, w_in, wa, wb)


def _mix2_kernel(m_ref, x_ref, wo_ref, gpost_ref, gpre_ref, h_ref, xn_ref):
    for rows in _tail_chunks(m_ref.shape[0]):
        m2 = jnp.dot(m_ref[rows, :], wo_ref[...], preferred_element_type=F32)
        h = x_ref[rows, :] + _rms(m2, gpost_ref[...])
        h_ref[rows, :] = h
        xn_ref[rows, :] = _rms(h, gpre_ref[...]).astype(BF16)


def _mix2(mixed, x2d, wo, g_post, g_pre_ffn):
    t, d = x2d.shape
    tm = MIX2_TM
    row_spec = pl.BlockSpec((tm, d), lambda i: (i, 0))
    return pl.pallas_call(
        _mix2_kernel,
        out_shape=(jax.ShapeDtypeStruct((t, d), F32), jax.ShapeDtypeStruct((t, d), BF16)),
        grid=(_exact_div(t, tm),),
        in_specs=[row_spec, row_spec, _resident(wo.shape), _resident((1, d)), _resident((1, d))],
        out_specs=(row_spec, row_spec),
        compiler_params=pltpu.CompilerParams(
            dimension_semantics=("parallel",),
            vmem_limit_bytes=VMEM_LIMIT),
        name="mix2",
    )(mixed, x2d, wo, g_post, g_pre_ffn)


def _ffn_kernel(xn_ref, wg_ref, wu_ref, wo_ref, f_ref, acc_ref):
    j = pl.program_id(1)

    @pl.when((pl.program_id(0) == 0) & (j == 0))
    def _():
        acc_ref[...] = jnp.zeros_like(acc_ref)

    xn = xn_ref[...]
    g = jnp.dot(xn, wg_ref[...], preferred_element_type=F32)
    u = jnp.dot(xn, wu_ref[...], preferred_element_type=F32)
    act = (g * _sigmoid(g) * u).astype(BF16)
    prev = jnp.where(j == 0, 0.0, acc_ref[...])
    acc = prev + jnp.dot(act, wo_ref[...], preferred_element_type=F32)
    acc_ref[...] = acc
    f_ref[...] = acc.astype(BF16)


def _ffn(xn2, w_ffn_in, w_ffn_out):
    t, d = xn2.shape
    d_ff = w_ffn_out.shape[0]
    tm = FFN_TM
    n_f = _exact_div(d_ff, FFN_TF)
    return pl.pallas_call(
        _ffn_kernel,
        out_shape=jax.ShapeDtypeStruct((t, d), BF16),
        grid=(_exact_div(t, tm), n_f),
        in_specs=[
            pl.BlockSpec((tm, d), lambda i, j: (i, 0)),
            pl.BlockSpec((d, FFN_TF), lambda i, j: (0, j)),
            pl.BlockSpec((d, FFN_TF), lambda i, j: (0, n_f + j)),
            pl.BlockSpec((FFN_TF, d), lambda i, j: (j, 0)),
        ],
        out_specs=pl.BlockSpec((tm, d), lambda i, j: (i, 0)),
        scratch_shapes=[pltpu.VMEM((tm, d), F32)],
        compiler_params=pltpu.CompilerParams(
            dimension_semantics=("arbitrary", "arbitrary"),
            vmem_limit_bytes=VMEM_LIMIT),
        name="ffn",
    )(xn2, w_ffn_in, w_ffn_in, w_ffn_out)


def _ple_kernel(h_ref, f_ref, p_ref, wple_ref, wgate_ref, gffn_ref, gple_ref, y_ref):
    for rows in _row_chunks(h_ref.shape[0]):
        h = h_ref[rows, :] + _rms(f_ref[rows, :].astype(F32), gffn_ref[...])
        e = jnp.dot(p_ref[rows, :].astype(BF16), wple_ref[...], preferred_element_type=F32)
        gate = _sigmoid(jnp.dot(h.astype(BF16), wgate_ref[...], preferred_element_type=F32))
        y_ref[rows, :] = h + _rms(gate * e, gple_ref[...])


def _ple(h1, f, p2d, w_ple, w_gate, g_ffn, g_ple):
    t, d = h1.shape
    pd = p2d.shape[1]
    tm = PLE_TM
    return pl.pallas_call(
        _ple_kernel,
        out_shape=jax.ShapeDtypeStruct((t, d), F32),
        grid=(_exact_div(t, tm),),
        in_specs=[
            pl.BlockSpec((tm, d), lambda i: (i, 0)),
            pl.BlockSpec((tm, d), lambda i: (i, 0)),
            pl.BlockSpec((tm, pd), lambda i: (i, 0)),
            _resident(w_ple.shape),
            _resident(w_gate.shape),
            _resident((1, d)),
            _resident((1, d)),
        ],
        out_specs=pl.BlockSpec((tm, d), lambda i: (i, 0)),
        compiler_params=pltpu.CompilerParams(
            dimension_semantics=("parallel",),
            vmem_limit_bytes=VMEM_LIMIT),
        name="ple",
    )(h1, f, p2d, w_ple, w_gate, g_ffn, g_ple)


def _rope_tables(seq):
    half = HEAD_DIM // 2
    t = np.arange(seq)
    inv_freq = ROPE_THETA ** (-np.arange(0, half, 2, dtype=np.float64) / half)

    def tables(pos):
        ang = pos.astype(np.float64)[:, None] * inv_freq[None, :]
        return (np.concatenate([np.cos(ang), np.cos(ang)], axis=-1),
                np.concatenate([np.sin(ang), np.sin(ang)], axis=-1))

    cos_r, sin_r = tables(t // GRID_W)
    cos_c, sin_c = tables(t % GRID_W)
    cos = np.concatenate([cos_r, cos_c], axis=-1)
    sin = np.concatenate([sin_r, sin_c], axis=-1)
    first = (np.arange(HEAD_DIM) % half) < half // 2
    sin_lo = np.where(first[None, :], -sin, 0.0)
    sin_hi = np.where(first[None, :], 0.0, sin)
    return tuple(jnp.asarray(a, F32) for a in (cos, sin_lo, sin_hi))


def _layer(x, p, w, seq):
    batch = x.shape[0]
    d = x.shape[-1]
    x2d = x.reshape(batch * seq, d)
    p2d = p.reshape(batch * seq, p.shape[-1])
    xn, pb = _qkv_b(x2d, w["g_pre_mix"], w["w_in"], w["cos"], w["sin_lo"], w["sin_hi"],
                    w["g_q"], w["g_k"], seq)
    pa = _xn_proj(_colscale_kernel, "qkv_a", xn, w["w_in"], N_ATTN_A, w["qkv_a_scale"])
    na_out = _na_attention(pa, w["na_tables"], batch, seq)
    gqa_out = _gqa_attention(pb, batch, seq)
    mixed = _mix1(xn, na_out, gqa_out, w["w_in"], w["w_branch_a"], w["w_branch_b"])
    h1, xn2 = _mix2(mixed, x2d, w["w_out"], w["g_post_mix"], w["g_pre_ffn"])
    f = _ffn(xn2, w["w_ffn_in"], w["w_ffn_out"])
    y = _ple(h1, f, p2d, w["w_ple"], w["w_ple_gate"], w["g_post_ffn"], w["g_ple"])
    return y.reshape(x.shape)


def kernel(x_prompt, x_sample, p_prompt, p_sample, g_pre_mix, w_in, rpb, g_q, g_k, w_branch_a, w_branch_b, w_out, g_post_mix, g_pre_ffn, w_ffn_in, w_ffn_out, g_post_ffn, w_ple, w_ple_gate, g_ple):
    depth = w_in.shape[0]
    h_p, h_s = x_prompt, x_sample
    qkv_a_scale = jnp.concatenate(
        [jnp.full((1, NA_WIDTH), Q_SCALE, F32), jnp.ones((1, 2 * NA_WIDTH), F32)], axis=1)
    for i in range(depth):
        w = {
            "w_in": w_in[i].astype(BF16),
            "qkv_a_scale": qkv_a_scale,
            "w_branch_a": w_branch_a[i].astype(BF16),
            "w_branch_b": w_branch_b[i].astype(BF16),
            "w_out": w_out[i].astype(BF16),
            "w_ffn_in": w_ffn_in[i].astype(BF16),
            "w_ffn_out": w_ffn_out[i].astype(BF16),
            "w_ple": w_ple[i].astype(BF16),
            "w_ple_gate": w_ple_gate[i].astype(BF16),
            "g_pre_mix": g_pre_mix[i][None, :],
            "g_q": g_q[i][None, :],
            "g_k": g_k[i][None, :],
            "g_post_mix": g_post_mix[i][None, :],
            "g_pre_ffn": g_pre_ffn[i][None, :],
            "g_post_ffn": g_post_ffn[i][None, :],
            "g_ple": g_ple[i][None, :],
        }
        outs = []
        for h, p in ((h_p, p_prompt[i]), (h_s, p_sample[i])):
            seq = h.shape[1]
            w["cos"], w["sin_lo"], w["sin_hi"] = _rope_tables(seq)
            w["na_tables"] = _na_bias_tables(rpb[i], seq // GRID_W)
            outs.append(_layer(h, p, w, seq))
        h_p, h_s = outs
    return (h_p, h_s)
```

```python
import functools
import math

import numpy as np
import jax
import jax.numpy as jnp
from jax import lax
from jax.experimental import pallas as pl
from jax.experimental.pallas import tpu as pltpu

F32 = jnp.float32
BF16 = jnp.bfloat16

HEAD_DIM = 128
NA_HEADS = 8
GQA_Q_HEADS = 8
GQA_KV_HEADS = 2
GQA_GROUP = GQA_Q_HEADS // GQA_KV_HEADS
GRID_W = 64
NA_ROWS = 8
NA_COLS = 16
ROPE_THETA = 10000.0
RMS_EPS = 1e-6
NEG_INF = -1e30
LOG2E = math.log2(math.e)
Q_SCALE = LOG2E / math.sqrt(HEAD_DIM)

NA_WIDTH = NA_HEADS * HEAD_DIM
GQA_Q_WIDTH = GQA_Q_HEADS * HEAD_DIM
GQA_KV_WIDTH = GQA_KV_HEADS * HEAD_DIM
N_ATTN_A = 3 * NA_WIDTH
N_ATTN_B = GQA_Q_WIDTH + 2 * GQA_KV_WIDTH

NA_QROWS = 4
NA_WIN_ROWS = 12
NA_TQ = NA_QROWS * GRID_W
NA_TK = NA_WIN_ROWS * GRID_W
NA_BLOCKS_PER_STEP = 8

V7X_VMEM_BYTES = 64 * 1024 * 1024
V7X_MXU_DIM = 256
VMEM_LIMIT = V7X_VMEM_BYTES // 8 * 7

CHAIN_ROWS = V7X_MXU_DIM
QKV_B_TM = 1024
XN_PROJ_TM, XN_PROJ_TN = 2048, 1024
GQA_TQ = 2048
GQA_CHAIN_ROWS = V7X_MXU_DIM // 2
MIX1_TM, MIX1_TN = 1024, 512
MIX2_TM = 512
FFN_TM, FFN_TF = 2048, 512
FFN_CHAIN_ROWS = 512
PLE_TM = 512


def _exact_div(a, b):
    assert a % b == 0, (a, b)
    return a // b


def _rms(x, g):
    ms = jnp.mean(x * x, axis=-1, keepdims=True)
    return x * lax.rsqrt(ms + RMS_EPS) * g


def _sigmoid(x):
    return 0.5 * jnp.tanh(0.5 * x) + 0.5


def _resident(shape):
    return pl.BlockSpec(shape, lambda *_: (0,) * len(shape), pipeline_mode=pl.Buffered(1))


def _rope(a, cos, sin_lo, sin_hi):
    return a * cos + pltpu.roll(a, 96, 1) * sin_lo + pltpu.roll(a, 32, 1) * sin_hi


def _row_chunks(n_rows):
    return [slice(r, r + CHAIN_ROWS) for r in range(0, n_rows, CHAIN_ROWS)]


def _tail_chunks(n_rows):
    last = CHAIN_ROWS // 2
    return [slice(0, n_rows - last), slice(n_rows - last, n_rows)]


def _qkv_b_kernel(x_ref, g_ref, w_ref, cos_ref, slo_ref, shi_ref, gq_ref, gk_ref,
                  xn_ref, o_ref):
    for rows in _row_chunks(x_ref.shape[0]):
        xn = _rms(x_ref[rows, :], g_ref[...]).astype(BF16)
        xn_ref[rows, :] = xn
        acc = jnp.dot(xn, w_ref[...], preferred_element_type=F32)
        cos, slo, shi = cos_ref[rows, :], slo_ref[rows, :], shi_ref[rows, :]
        for h in range(GQA_Q_HEADS + GQA_KV_HEADS):
            sl = slice(h * HEAD_DIM, (h + 1) * HEAD_DIM)
            if h < GQA_Q_HEADS:
                a = _rope(_rms(acc[:, sl], gq_ref[...]), cos, slo, shi) * Q_SCALE
            else:
                a = _rope(_rms(acc[:, sl], gk_ref[...]), cos, slo, shi)
            o_ref[rows, sl] = a.astype(BF16)
        v0 = GQA_Q_WIDTH + GQA_KV_WIDTH
        o_ref[rows, v0:] = acc[:, v0:].astype(BF16)


def _qkv_b(x2d, g_pre, w_in, cos, sin_lo, sin_hi, g_q, g_k, seq):
    t, d = x2d.shape
    tm = QKV_B_TM
    n_pos_blocks = _exact_div(seq, tm)
    pos_spec = pl.BlockSpec((tm, HEAD_DIM), lambda i: (i % n_pos_blocks, 0))
    return pl.pallas_call(
        _qkv_b_kernel,
        out_shape=(jax.ShapeDtypeStruct((t, d), BF16), jax.ShapeDtypeStruct((t, N_ATTN_B), BF16)),
        grid=(_exact_div(t, tm),),
        in_specs=[
            pl.BlockSpec((tm, d), lambda i: (i, 0)),
            _resident((1, d)),
            pl.BlockSpec((d, N_ATTN_B), lambda i: (0, _exact_div(N_ATTN_A, N_ATTN_B)),
                         pipeline_mode=pl.Buffered(1)),
            pos_spec, pos_spec, pos_spec,
            _resident((1, HEAD_DIM)),
            _resident((1, HEAD_DIM)),
        ],
        out_specs=(pl.BlockSpec((tm, d), lambda i: (i, 0)),
                   pl.BlockSpec((tm, N_ATTN_B), lambda i: (i, 0))),
        compiler_params=pltpu.CompilerParams(
            dimension_semantics=("parallel",),
            vmem_limit_bytes=VMEM_LIMIT),
        name="qkv_b",
    )(x2d, g_pre, w_in, cos, sin_lo, sin_hi, g_q, g_k)


def _colscale_kernel(xn_ref, w_ref, cs_ref, o_ref):
    acc = jnp.dot(xn_ref[...], w_ref[...], preferred_element_type=F32)
    o_ref[...] = (acc * cs_ref[...]).astype(BF16)


def _xn_proj(body, name, xn, w, n, *extra):
    t, d = xn.shape
    tm, tn = XN_PROJ_TM, XN_PROJ_TN
    extra_specs = [pl.BlockSpec((1, tn), lambda i, j: (0, j)) for _ in extra]
    return pl.pallas_call(
        body,
        out_shape=jax.ShapeDtypeStruct((t, n), BF16),
        grid=(_exact_div(t, tm), _exact_div(n, tn)),
        in_specs=[
            pl.BlockSpec((tm, d), lambda i, j: (i, 0)),
            pl.BlockSpec((d, tn), lambda i, j: (0, j)),
        ] + extra_specs,
        out_specs=pl.BlockSpec((tm, tn), lambda i, j: (i, j)),
        compiler_params=pltpu.CompilerParams(
            dimension_semantics=("parallel", "arbitrary"),
            vmem_limit_bytes=VMEM_LIMIT),
        name=name,
    )(xn, w, *extra)


def _na_window_start(blk, rows):
    return jnp.clip(blk * NA_QROWS - NA_ROWS // 2, 0, rows - NA_WIN_ROWS)


def _na_kernel(q_ref, k_ref, v_ref, tab_ref, o_ref, *, rows):
    n_blk = rows // NA_QROWS
    ones = jnp.ones((NA_TK, HEAD_DIM), BF16)
    for i in range(NA_BLOCKS_PER_STEP):
        blk = pl.program_id(1) * NA_BLOCKS_PER_STEP + i
        qrows = slice(i * NA_TQ, (i + 1) * NA_TQ)
        k0 = pl.multiple_of(_na_window_start(blk, rows) * GRID_W, GRID_W)
        variant = jnp.where(blk == 0, 0, jnp.where(blk == n_blk - 1, 2, 1))
        for h in range(NA_HEADS):
            sl = slice(h * HEAD_DIM, (h + 1) * HEAD_DIM)
            q = q_ref[qrows, sl]
            k = k_ref[pl.ds(k0, NA_TK), sl]
            v = v_ref[pl.ds(k0, NA_TK), sl]
            s = lax.dot_general(q, k, (((1,), (1,)), ((), ())), preferred_element_type=F32)
            s = s + tab_ref[variant, h]
            m = jnp.max(s, axis=-1, keepdims=True)
            p = jnp.exp2(s - m).astype(BF16)
            ol = jnp.dot(p, jnp.concatenate([v, ones], axis=1), preferred_element_type=F32)
            o_ref[qrows, sl] = (ol[:, :HEAD_DIM] * (1.0 / ol[:, HEAD_DIM:])).astype(BF16)


def _na_bias_tables(rpb, rows):
    n_blk = rows // NA_QROWS
    qc = np.arange(GRID_W)[:, None]
    kc = np.arange(GRID_W)[None, :]
    dc = np.clip(kc - qc, -(NA_COLS - 1), NA_COLS - 1) + NA_COLS - 1
    onehot = (dc[None] == np.arange(2 * NA_COLS - 1)[:, None, None]).astype(np.float32)
    cexp = jnp.einsum("hrd,dqk->hqrk", rpb.astype(F32) * LOG2E, jnp.asarray(onehot),
                      precision=lax.Precision.HIGHEST)
    pad = NA_WIN_ROWS
    cexp = jnp.pad(cexp, ((0, 0), (0, 0), (pad, pad), (0, 0)))
    cs = np.clip(qc - NA_COLS // 2, 0, GRID_W - NA_COLS)
    col_ok = (kc >= cs) & (kc < cs + NA_COLS)
    tabs = []
    for blk in (0, 1, n_blk - 1):
        r0 = blk * NA_QROWS
        k0 = int(np.clip(r0 - NA_ROWS // 2, 0, rows - NA_WIN_ROWS))
        per_row = []
        for qr in range(r0, r0 + NA_QROWS):
            start = k0 - qr + NA_ROWS - 1 + pad
            per_row.append(cexp[:, :, start:start + NA_WIN_ROWS])
        b = jnp.stack(per_row, axis=1)
        qr_ = (r0 + np.arange(NA_QROWS))[:, None]
        kr_ = (k0 + np.arange(NA_WIN_ROWS))[None, :]
        rs = np.clip(qr_ - NA_ROWS // 2, 0, rows - NA_ROWS)
        row_ok = (kr_ >= rs) & (kr_ < rs + NA_ROWS)
        ok = row_ok[:, None, :, None] & col_ok[None, :, None, :]
        b = jnp.where(jnp.asarray(ok)[None], b, NEG_INF)
        tabs.append(b.reshape(NA_HEADS, NA_TQ, NA_TK))
    return jnp.stack(tabs)


def _na_attention(pa, tables, batch, seq):
    rows = seq // GRID_W
    n_steps = _exact_div(rows, NA_QROWS * NA_BLOCKS_PER_STEP)
    tq = NA_TQ * NA_BLOCKS_PER_STEP
    t = batch * seq

    return pl.pallas_call(
        functools.partial(_na_kernel, rows=rows),
        out_shape=jax.ShapeDtypeStruct((t, NA_WIDTH), BF16),
        grid=(batch, n_steps),
        in_specs=[
            pl.BlockSpec((tq, NA_WIDTH), lambda b, i: (b * n_steps + i, 0)),
            pl.BlockSpec((seq, NA_WIDTH), lambda b, i: (b, 1)),
            pl.BlockSpec((seq, NA_WIDTH), lambda b, i: (b, 2)),
            _resident(tables.shape),
        ],
        out_specs=pl.BlockSpec((tq, NA_WIDTH), lambda b, i: (b * n_steps + i, 0)),
        compiler_params=pltpu.CompilerParams(
            dimension_semantics=("parallel", "arbitrary"),
            vmem_limit_bytes=VMEM_LIMIT),
        name="na_attn",
    )(pa, pa, pa, tables)


def _gqa_kernel(q_ref, k_ref, v_ref, o_ref, vaug_ref):
    @pl.when(pl.program_id(2) == 0)
    def _():
        vaug_ref[:, :HEAD_DIM] = v_ref[...]
        vaug_ref[:, HEAD_DIM:] = jnp.ones((v_ref.shape[0], HEAD_DIM), BF16)

    k = k_ref[...]
    for r in range(GQA_TQ // GQA_CHAIN_ROWS):
        rows = slice(r * GQA_CHAIN_ROWS, (r + 1) * GQA_CHAIN_ROWS)
        for h in range(GQA_GROUP):
            sl = slice(h * HEAD_DIM, (h + 1) * HEAD_DIM)
            s = lax.dot_general(q_ref[rows, sl], k, (((1,), (1,)), ((), ())),
                                preferred_element_type=F32)
            m = jnp.max(s, axis=-1, keepdims=True)
            p = jnp.exp2(s - m).astype(BF16)
            ol = jnp.dot(p, vaug_ref[...], preferred_element_type=F32)
            o_ref[rows, sl] = (ol[:, :HEAD_DIM] * (1.0 / ol[:, HEAD_DIM:])).astype(BF16)


def _gqa_attention(pb, batch, seq):
    t = batch * seq
    n_q = _exact_div(seq, GQA_TQ)
    gw = GQA_GROUP * HEAD_DIM
    return pl.pallas_call(
        _gqa_kernel,
        out_shape=jax.ShapeDtypeStruct((t, GQA_Q_WIDTH), BF16),
        grid=(batch, GQA_KV_HEADS, n_q),
        in_specs=[
            pl.BlockSpec((GQA_TQ, gw), lambda b, g, i: (b * n_q + i, g)),
            pl.BlockSpec((seq, HEAD_DIM), lambda b, g, i: (b, GQA_Q_WIDTH // HEAD_DIM + g)),
            pl.BlockSpec((seq, HEAD_DIM),
                         lambda b, g, i: (b, (GQA_Q_WIDTH + GQA_KV_WIDTH) // HEAD_DIM + g)),
        ],
        out_specs=pl.BlockSpec((GQA_TQ, gw), lambda b, g, i: (b * n_q + i, g)),
        scratch_shapes=[pltpu.VMEM((seq, 2 * HEAD_DIM), BF16)],
        compiler_params=pltpu.CompilerParams(
            dimension_semantics=("parallel", "parallel", "arbitrary"),
            vmem_limit_bytes=VMEM_LIMIT),
        name="gqa_attn",
    )(pb, pb, pb)


def _mix1_kernel(xn_ref, a_ref, b_ref, wga_ref, wgb_ref, wa_ref, wb_ref, o_ref):
    xn = xn_ref[...]
    ga = _sigmoid(jnp.dot(xn, wga_ref[...], preferred_element_type=F32))
    ya = jnp.dot(a_ref[...], wa_ref[...], preferred_element_type=F32)
    gb = _sigmoid(jnp.dot(xn, wgb_ref[...], preferred_element_type=F32))
    yb = jnp.dot(b_ref[...], wb_ref[...], preferred_element_type=F32)
    o_ref[...] = (ga * ya + gb * yb).astype(BF16)


def _mix1(xn, na_out, gqa_out, w_in, wa, wb):
    t, d = xn.shape
    tm, tn = MIX1_TM, MIX1_TN
    n_j = _exact_div(d, tn)
    j0 = _exact_div(N_ATTN_A + N_ATTN_B, tn)
    return pl.pallas_call(
        _mix1_kernel,
        out_shape=jax.ShapeDtypeStruct((t, d), BF16),
        grid=(_exact_div(t, tm), n_j),
        in_specs=[
            pl.BlockSpec((tm, d), lambda i, j: (i, 0)),
            pl.BlockSpec((tm, NA_WIDTH), lambda i, j: (i, 0)),
            pl.BlockSpec((tm, GQA_Q_WIDTH), lambda i, j: (i, 0)),
            pl.BlockSpec((d, tn), lambda i, j: (0, j0 + j)),
            pl.BlockSpec((d, tn), lambda i, j: (0, j0 + n_j + j)),
            pl.BlockSpec((NA_WIDTH, tn), lambda i, j: (0, j)),
            pl.BlockSpec((GQA_Q_WIDTH, tn), lambda i, j: (0, j)),
        ],
        out_specs=pl.BlockSpec((tm, tn), lambda i, j: (i, j)),
        compiler_params=pltpu.CompilerParams(
            dimension_semantics=("parallel", "arbitrary"),
            vmem_limit_bytes=VMEM_LIMIT),
        name="mix1",
    )(xn, na_out, gqa_out, w_in, w_in, wa, wb)


def _mix2_kernel(m_ref, x_ref, wo_ref, gpost_ref, gpre_ref, h_ref, xn_ref):
    for rows in _tail_chunks(m_ref.shape[0]):
        m2 = jnp.dot(m_ref[rows, :], wo_ref[...], preferred_element_type=F32)
        h = x_ref[rows, :] + _rms(m2, gpost_ref[...])
        h_ref[rows, :] = h
        xn_ref[rows, :] = _rms(h, gpre_ref[...]).astype(BF16)


def _mix2(mixed, x2d, wo, g_post, g_pre_ffn):
    t, d = x2d.shape
    tm = MIX2_TM
    row_spec = pl.BlockSpec((tm, d), lambda i: (i, 0))
    return pl.pallas_call(
        _mix2_kernel,
        out_shape=(jax.ShapeDtypeStruct((t, d), F32), jax.ShapeDtypeStruct((t, d), BF16)),
        grid=(_exact_div(t, tm),),
        in_specs=[row_spec, row_spec, _resident(wo.shape), _resident((1, d)), _resident((1, d))],
        out_specs=(row_spec, row_spec),
        compiler_params=pltpu.CompilerParams(
            dimension_semantics=("parallel",),
            vmem_limit_bytes=VMEM_LIMIT),
        name="mix2",
    )(mixed, x2d, wo, g_post, g_pre_ffn)


def _ffn_kernel(xn_ref, wg_ref, wu_ref, wo_ref, f_ref, acc_ref):
    j = pl.program_id(1)

    @pl.when((pl.program_id(0) == 0) & (j == 0))
    def _():
        acc_ref[...] = jnp.zeros_like(acc_ref)

    for r in range(0, xn_ref.shape[0], FFN_CHAIN_ROWS):
        rows = slice(r, r + FFN_CHAIN_ROWS)
        xn = xn_ref[rows, :]
        g = jnp.dot(xn, wg_ref[...], preferred_element_type=F32)
        u = jnp.dot(xn, wu_ref[...], preferred_element_type=F32)
        act = (g * _sigmoid(g) * u).astype(BF16)
        prev = jnp.where(j == 0, 0.0, acc_ref[rows, :])
        acc = prev + jnp.dot(act, wo_ref[...], preferred_element_type=F32)
        acc_ref[rows, :] = acc
        f_ref[rows, :] = acc.astype(BF16)


def _ffn(xn2, w_ffn_in, w_ffn_out):
    t, d = xn2.shape
    d_ff = w_ffn_out.shape[0]
    tm = FFN_TM
    n_f = _exact_div(d_ff, FFN_TF)
    return pl.pallas_call(
        _ffn_kernel,
        out_shape=jax.ShapeDtypeStruct((t, d), BF16),
        grid=(_exact_div(t, tm), n_f),
        in_specs=[
            pl.BlockSpec((tm, d), lambda i, j: (i, 0), pipeline_mode=pl.Buffered(1)),
            pl.BlockSpec((d, FFN_TF), lambda i, j: (0, j)),
            pl.BlockSpec((d, FFN_TF), lambda i, j: (0, n_f + j)),
            pl.BlockSpec((FFN_TF, d), lambda i, j: (j, 0)),
        ],
        out_specs=pl.BlockSpec((tm, d), lambda i, j: (i, 0), pipeline_mode=pl.Buffered(1)),
        scratch_shapes=[pltpu.VMEM((tm, d), F32)],
        compiler_params=pltpu.CompilerParams(
            dimension_semantics=("arbitrary", "arbitrary"),
            vmem_limit_bytes=VMEM_LIMIT),
        name="ffn",
    )(xn2, w_ffn_in, w_ffn_in, w_ffn_out)


def _ple_kernel(h_ref, f_ref, p_ref, wple_ref, wgate_ref, gffn_ref, gple_ref, y_ref):
    for rows in _row_chunks(h_ref.shape[0]):
        h = h_ref[rows, :] + _rms(f_ref[rows, :].astype(F32), gffn_ref[...])
        e = jnp.dot(p_ref[rows, :].astype(BF16), wple_ref[...], preferred_element_type=F32)
        gate = _sigmoid(jnp.dot(h.astype(BF16), wgate_ref[...], preferred_element_type=F32))
        y_ref[rows, :] = h + _rms(gate * e, gple_ref[...])


def _ple(h1, f, p2d, w_ple, w_gate, g_ffn, g_ple):
    t, d = h1.shape
    pd = p2d.shape[1]
    tm = PLE_TM
    return pl.pallas_call(
        _ple_kernel,
        out_shape=jax.ShapeDtypeStruct((t, d), F32),
        grid=(_exact_div(t, tm),),
        in_specs=[
            pl.BlockSpec((tm, d), lambda i: (i, 0)),
            pl.BlockSpec((tm, d), lambda i: (i, 0)),
            pl.BlockSpec((tm, pd), lambda i: (i, 0)),
            _resident(w_ple.shape),
            _resident(w_gate.shape),
            _resident((1, d)),
            _resident((1, d)),
        ],
        out_specs=pl.BlockSpec((tm, d), lambda i: (i, 0)),
        compiler_params=pltpu.CompilerParams(
            dimension_semantics=("parallel",),
            vmem_limit_bytes=VMEM_LIMIT),
        name="ple",
    )(h1, f, p2d, w_ple, w_gate, g_ffn, g_ple)


def _rope_tables(seq):
    half = HEAD_DIM // 2
    t = np.arange(seq)
    inv_freq = ROPE_THETA ** (-np.arange(0, half, 2, dtype=np.float64) / half)

    def tables(pos):
        ang = pos.astype(np.float64)[:, None] * inv_freq[None, :]
        return (np.concatenate([np.cos(ang), np.cos(ang)], axis=-1),
                np.concatenate([np.sin(ang), np.sin(ang)], axis=-1))

    cos_r, sin_r = tables(t // GRID_W)
    cos_c, sin_c = tables(t % GRID_W)
    cos = np.concatenate([cos_r, cos_c], axis=-1)
    sin = np.concatenate([sin_r, sin_c], axis=-1)
    first = (np.arange(HEAD_DIM) % half) < half // 2
    sin_lo = np.where(first[None, :], -sin, 0.0)
    sin_hi = np.where(first[None, :], 0.0, sin)
    return tuple(jnp.asarray(a, F32) for a in (cos, sin_lo, sin_hi))


def _layer(x, p, w, seq):
    batch = x.shape[0]
    d = x.shape[-1]
    x2d = x.reshape(batch * seq, d)
    p2d = p.reshape(batch * seq, p.shape[-1])
    xn, pb = _qkv_b(x2d, w["g_pre_mix"], w["w_in"], w["cos"], w["sin_lo"], w["sin_hi"],
                    w["g_q"], w["g_k"], seq)
    pa = _xn_proj(_colscale_kernel, "qkv_a", xn, w["w_in"], N_ATTN_A, w["qkv_a_scale"])
    na_out = _na_attention(pa, w["na_tables"], batch, seq)
    gqa_out = _gqa_attention(pb, batch, seq)
    mixed = _mix1(xn, na_out, gqa_out, w["w_in"], w["w_branch_a"], w["w_branch_b"])
    h1, xn2 = _mix2(mixed, x2d, w["w_out"], w["g_post_mix"], w["g_pre_ffn"])
    f = _ffn(xn2, w["w_ffn_in"], w["w_ffn_out"])
    y = _ple(h1, f, p2d, w["w_ple"], w["w_ple_gate"], w["g_post_ffn"], w["g_ple"])
    return y.reshape(x.shape)


def kernel(x_prompt, x_sample, p_prompt, p_sample, g_pre_mix, w_in, rpb, g_q, g_k, w_branch_a, w_branch_b, w_out, g_post_mix, g_pre_ffn, w_ffn_in, w_ffn_out, g_post_ffn, w_ple, w_ple_gate, g_ple):
    depth = w_in.shape[0]
    h_p, h_s = x_prompt, x_sample
    qkv_a_scale = jnp.concatenate(
        [jnp.full((1, NA_WIDTH), Q_SCALE, F32), jnp.ones((1, 2 * NA_WIDTH), F32)], axis=1)
    for i in range(depth):
        w = {
            "w_in": w_in[i].astype(BF16),
            "qkv_a_scale": qkv_a_scale,
            "w_branch_a": w_branch_a[i].astype(BF16),
            "w_branch_b": w_branch_b[i].astype(BF16),
            "w_out": w_out[i].astype(BF16),
            "w_ffn_in": w_ffn_in[i].astype(BF16),
            "w_ffn_out": w_ffn_out[i].astype(BF16),
            "w_ple": w_ple[i].astype(BF16),
            "w_ple_gate": w_ple_gate[i].astype(BF16),
            "g_pre_mix": g_pre_mix[i][None, :],
            "g_q": g_q[i][None, :],
            "g_k": g_k[i][None, :],
            "g_post_mix": g_post_mix[i][None, :],
            "g_pre_ffn": g_pre_ffn[i][None, :],
            "g_post_ffn": g_post_ffn[i][None, :],
            "g_ple": g_ple[i][None, :],
        }
        outs = []
        for h, p in ((h_p, p_prompt[i]), (h_s, p_sample[i])):
            seq = h.shape[1]
            w["cos"], w["sin_lo"], w["sin_hi"] = _rope_tables(seq)
            w["na_tables"] = _na_bias_tables(rpb[i], seq // GRID_W)
            outs.append(_layer(h, p, w, seq))
        h_p, h_s = outs
    return (h_p, h_s)
```
